```python
import math
import jax, jax.numpy as jnp
from jax import lax
import numpy as np

D_MODEL = 1024
BATCH = 32
SEQ = 2048
DEPTH = 1
DEC_BATCH = 16
DEC_SEQ = 2048
PAST_LEN = 128

HEAD_DIM = 64
HEADS_PER_GROUP = 4
ATTN_PATTERNS = ((128, 1), (512, 4), (2048, 16))
N_ATTN_HEADS = HEADS_PER_GROUP * len(ATTN_PATTERNS)
ATTN_WIDTH = N_ATTN_HEADS * HEAD_DIM
ATTN_OUT_WIDTH = HEADS_PER_GROUP * HEAD_DIM
ROPE_DIM = HEAD_DIM // 4
ROPE_THETA = 500000.0
POOL_WINDOWS = (2, 4, 8, 16)
POOL_WIDTH = D_MODEL // 2
POOL_GROUP = POOL_WIDTH // len(POOL_WINDOWS)
IN_WIDTH = POOL_WIDTH + 3 * ATTN_WIDTH
D_FF = -(-8 * D_MODEL // (3 * 256)) * 256
BLOCK_Q = 64
NORM_EPS = 1e-6

kernel_name = "gated_pool_dilated_attn_encoder"


def rmsnorm(x, g):
    xf = x.astype(jnp.float32)
    y = xf * lax.rsqrt(jnp.mean(xf * xf, axis=-1, keepdims=True) + NORM_EPS) * g.astype(jnp.float32)
    return y.astype(x.dtype)


def partial_rope(t, pos):
    half = ROPE_DIM // 2
    inv = jnp.asarray(ROPE_THETA ** (-np.arange(half, dtype=np.float32) / half), dtype=jnp.float32)
    ang = pos[:, None] * inv[None, :]
    cos = jnp.cos(ang)[None, :, None, :]
    sin = jnp.sin(ang)[None, :, None, :]
    tf = t.astype(jnp.float32)
    x1 = tf[..., :half]
    x2 = tf[..., half:ROPE_DIM]
    rot = jnp.concatenate([x1 * cos - x2 * sin, x2 * cos + x1 * sin, tf[..., ROPE_DIM:]], axis=-1)
    return rot.astype(t.dtype)


def pool_mixer(u, maps, scale):
    B, S, _ = u.shape
    uf = u.astype(jnp.float32)
    c = jnp.pad(jnp.cumsum(uf, axis=1), ((0, 0), (1, 0), (0, 0)))
    pos = np.arange(S)
    outs = []
    for g, w in enumerate(POOL_WINDOWS):
        sl = slice(g * POOL_GROUP, (g + 1) * POOL_GROUP)
        lo = np.clip(pos - w // 2, 0, S)
        hi = np.clip(pos + w // 2, 0, S)
        cnt = jnp.asarray((hi - lo).astype(np.float32))[None, :, None]
        cg = c[..., sl]
        outs.append((cg[:, hi] - cg[:, lo]) / cnt - uf[..., sl])
    pooled = jnp.stack(outs, axis=2)
    mixed = jnp.einsum('bsgc,gcd->bsgd', pooled, maps.astype(jnp.float32)).reshape(B, S, POOL_WIDTH)
    return (mixed * scale.astype(jnp.float32)).astype(u.dtype)


def dilated_window_attention(q, k, v, window, dilation):
    B, S, H, E = q.shape
    d = dilation
    n = S // d
    R = (window // 2) // d
    qb = math.gcd(n, BLOCK_Q)
    nb = n // qb
    span = qb + 2 * R

    def to_sub(t):
        return t.reshape(B, n, d, H, E).transpose(0, 2, 1, 3, 4)

    qs = to_sub(q).reshape(B, d, nb, qb, H, E)
    pad = ((0, 0), (0, 0), (R, R), (0, 0), (0, 0))
    kp = jnp.pad(to_sub(k), pad)
    vp = jnp.pad(to_sub(v), pad)
    kidx = np.arange(nb)[:, None] * qb + np.arange(span)[None, :]
    kb = kp[:, :, kidx]
    vb = vp[:, :, kidx]
    s = jnp.einsum('bdnqhe,bdnkhe->bdnhqk', qs, kb, preferred_element_type=jnp.float32) * (E ** -0.5)
    rel = np.arange(span)[None, :] - R - np.arange(qb)[:, None]
    key_sub = kidx - R
    mask = (np.abs(rel) <= R)[None] & ((key_sub >= 0) & (key_sub < n))[:, None, :]
    s = jnp.where(jnp.asarray(mask)[None, None, :, None], s, -jnp.inf)
    m = jnp.max(s, axis=-1, keepdims=True)
    p = jnp.exp(s - m)
    den = jnp.sum(p, axis=-1, keepdims=True)
    o = jnp.einsum('bdnhqk,bdnkhe->bdnqhe', p, vb.astype(jnp.float32)) / den.transpose(0, 1, 2, 4, 3, 5)
    lse = (m + jnp.log(den))[..., 0].transpose(0, 1, 2, 4, 3)
    o = o.reshape(B, d, n, H, E).transpose(0, 2, 1, 3, 4).reshape(B, S, H, E)
    lse = lse.reshape(B, d, n, H).transpose(0, 2, 1, 3).reshape(B, S, H)
    return o, lse


def attention_branch(z_attn):
    B, S, _ = z_attn.shape
    q = z_attn[..., :ATTN_WIDTH].reshape(B, S, N_ATTN_HEADS, HEAD_DIM)
    k = z_attn[..., ATTN_WIDTH:2 * ATTN_WIDTH].reshape(B, S, N_ATTN_HEADS, HEAD_DIM)
    v = z_attn[..., 2 * ATTN_WIDTH:].reshape(B, S, N_ATTN_HEADS, HEAD_DIM)
    pos = jnp.arange(S, dtype=jnp.float32)
    q = partial_rope(q, pos)
    k = partial_rope(k, pos)
    outs, lses = [], []
    for g, (w, dil) in enumerate(ATTN_PATTERNS):
        hs = slice(g * HEADS_PER_GROUP, (g + 1) * HEADS_PER_GROUP)
        o, l = dilated_window_attention(q[:, :, hs], k[:, :, hs], v[:, :, hs], w, dil)
        outs.append(o)
        lses.append(l)
    outs = jnp.stack(outs, axis=0)
    alpha = jax.nn.softmax(jnp.stack(lses, axis=0), axis=0)
    merged = jnp.sum(alpha[..., None] * outs, axis=0).reshape(B, S, ATTN_OUT_WIDTH)
    return merged.astype(z_attn.dtype)


def encoder_forward(x, norm_mix, w_in, pool_maps, pool_scale, w_up_pool, w_up_attn, w_gate, b_gate,
                    w_out, norm_ffn, w_ffn_gate, w_ffn_up, w_ffn_down, norm_final):
    for l in range(DEPTH):
        h = rmsnorm(x, norm_mix[l])
        z = h @ w_in[l]
        a = pool_mixer(z[..., :POOL_WIDTH], pool_maps[l], pool_scale[l]) @ w_up_pool[l]
        b = attention_branch(z[..., POOL_WIDTH:]) @ w_up_attn[l]
        gates = jax.nn.sigmoid(h @ w_gate[l] + b_gate[l])
        g_a = gates[..., :D_MODEL]
        g_b = gates[..., D_MODEL:]
        x = x + (g_a * a + g_b * b) @ w_out[l]
        h = rmsnorm(x, norm_ffn[l])
        x = x + (jax.nn.silu(h @ w_ffn_gate[l]) * (h @ w_ffn_up[l])) @ w_ffn_down[l]
    return rmsnorm(x, norm_final)


def setup_inputs(seed: int = 0) -> dict:
    key = jax.random.key(seed)
    ks = jax.random.split(key, 18)
    f32 = jnp.float32

    def dense(k, shape, fan_in):
        return jax.random.normal(k, shape, f32) * (fan_in ** -0.5)

    def gain(k, shape):
        return 1.0 + 0.05 * jax.random.normal(k, shape, f32)

    return {
        "x_prompt": jax.random.normal(ks[0], (BATCH, SEQ, D_MODEL), f32),
        "x_sample": jax.random.normal(ks[1], (DEC_BATCH, DEC_SEQ, D_MODEL), f32),
        "norm_mix": gain(ks[2], (DEPTH, D_MODEL)),
        "w_in": dense(ks[3], (DEPTH, D_MODEL, IN_WIDTH), D_MODEL),
        "pool_maps": dense(ks[4], (DEPTH, len(POOL_WINDOWS), POOL_GROUP, POOL_GROUP), POOL_GROUP),
        "pool_scale": gain(ks[5], (DEPTH, POOL_WIDTH)),
        "w_up_pool": dense(ks[6], (DEPTH, POOL_WIDTH, D_MODEL), POOL_WIDTH),
        "w_up_attn": dense(ks[7], (DEPTH, ATTN_OUT_WIDTH, D_MODEL), ATTN_OUT_WIDTH),
        "w_gate": dense(ks[8], (DEPTH, D_MODEL, 2 * D_MODEL), D_MODEL),
        "b_gate": 0.02 * jax.random.normal(ks[9], (DEPTH, 2 * D_MODEL), f32),
        "w_out": dense(ks[10], (DEPTH, D_MODEL, D_MODEL), D_MODEL),
        "norm_ffn": gain(ks[11], (DEPTH, D_MODEL)),
        "w_ffn_gate": dense(ks[12], (DEPTH, D_MODEL, D_FF), D_MODEL),
        "w_ffn_up": dense(ks[13], (DEPTH, D_MODEL, D_FF), D_MODEL),
        "w_ffn_down": dense(ks[14], (DEPTH, D_FF, D_MODEL), D_FF),
        "norm_final": gain(ks[15], (D_MODEL,)),
    }


def reference(x_prompt, x_sample, norm_mix, w_in, pool_maps, pool_scale, w_up_pool, w_up_attn, w_gate, b_gate,
              w_out, norm_ffn, w_ffn_gate, w_ffn_up, w_ffn_down, norm_final):
    y_prompt = encoder_forward(x_prompt, norm_mix, w_in, pool_maps, pool_scale, w_up_pool, w_up_attn, w_gate,
                               b_gate, w_out, norm_ffn, w_ffn_gate, w_ffn_up, w_ffn_down, norm_final)
    y_sample = encoder_forward(x_sample, norm_mix, w_in, pool_maps, pool_scale, w_up_pool, w_up_attn, w_gate,
                               b_gate, w_out, norm_ffn, w_ffn_gate, w_ffn_up, w_ffn_down, norm_final)
    return (y_prompt, y_sample)
```

```python
import functools

import jax
import jax.numpy as jnp
import numpy as np
from jax import lax
from jax.experimental import pallas as pl
from jax.experimental.pallas import tpu as pltpu

HEAD_DIM = 64
HEADS_PER_GROUP = 4
ATTN_PATTERNS = ((128, 1), (512, 4), (2048, 16))
N_GROUPS = len(ATTN_PATTERNS)
GROUP_WIDTH = HEADS_PER_GROUP * HEAD_DIM
ATTN_WIDTH = N_GROUPS * GROUP_WIDTH
ROPE_DIM = HEAD_DIM // 4
ROPE_THETA = 500000.0
POOL_WINDOWS = (2, 4, 8, 16)
POOL_GROUP = 128
POOL_WIDTH = POOL_GROUP * len(POOL_WINDOWS)
POOL_HALO = 8
NORM_EPS = 1e-6

LANES = 128
Q_BLOCK = 128
BAND = 64
KEY_SPAN = Q_BLOCK + 2 * BAND
PAIRS = GROUP_WIDTH // LANES
TOKEN_TILE = 512
VMEM_LIMIT_BYTES = 56 * 1024 * 1024

F32 = jnp.float32
BF16 = jnp.bfloat16


def _rms_scale(x, gain):
    return x * lax.rsqrt(jnp.mean(x * x, axis=-1, keepdims=True) + NORM_EPS) * gain


def _rope_tables(seq):
    half = ROPE_DIM // 2
    inv = (ROPE_THETA ** (-np.arange(half, dtype=np.float32) / half)).astype(np.float32).astype(np.float64)
    ang = np.arange(seq, dtype=np.float64)[:, None] * inv[None, :]
    cos, sin = np.cos(ang), np.sin(ang)
    c = np.ones((seq, HEAD_DIM))
    s_up = np.zeros((seq, HEAD_DIM))
    s_down = np.zeros((seq, HEAD_DIM))
    c[:, :half] = cos
    c[:, half:ROPE_DIM] = cos
    s_down[:, :half] = -sin
    s_up[:, half:ROPE_DIM] = sin
    rep = LANES // HEAD_DIM
    return tuple(jnp.asarray(np.tile(t, (1, rep)), dtype=F32) for t in (c, s_up, s_down))


def _in_proj_kernel(x_ref, g_ref, w_ref, c_ref, su_ref, sd_ref, u_ref, qkv1_ref, qkv2_ref, qkv3_ref, perm_ref):
    tm = x_ref.shape[1]
    h = _rms_scale(x_ref[0], g_ref[...]).astype(BF16)
    z = jnp.dot(h, w_ref[...], preferred_element_type=F32)
    u_ref[0] = z[:, :POOL_WIDTH]

    c, su, sd = c_ref[...], su_ref[...], sd_ref[...]
    half = ROPE_DIM // 2

    def rope(t):
        return t * c + pltpu.roll(t, half, 1) * su + pltpu.roll(t, LANES - half, 1) * sd

    scale = HEAD_DIM ** -0.5
    chunks = 3 * GROUP_WIDTH // LANES
    for g in range(N_GROUPS):
        for part in range(3):
            for j in range(GROUP_WIDTH // LANES):
                src = POOL_WIDTH + part * ATTN_WIDTH + g * GROUP_WIDTH + j * LANES
                t = z[:, src:src + LANES]
                if part < 2:
                    t = rope(t)
                if part == 0:
                    t = t * scale
                dst = part * GROUP_WIDTH + j * LANES
                if g == 0:
                    qkv1_ref[0, :, dst:dst + LANES] = t.astype(BF16)
                else:
                    perm_ref[(g - 1) * chunks + dst // LANES] = t

    for g, out_ref in ((1, qkv2_ref), (2, qkv3_ref)):
        d = ATTN_PATTERNS[g][1]
        for r in range(d):
            for ch in range(chunks):
                rows = perm_ref[(g - 1) * chunks + ch, pl.ds(r, tm // d, stride=d), :]
                out_ref[0, r, :, ch * LANES:(ch + 1) * LANES] = rows.astype(BF16)


def _in_proj(x, norm_g, w_in, rope_tabs):
    b, s, dm = x.shape
    tm = TOKEN_TILE
    d2, d3 = ATTN_PATTERNS[1][1], ATTN_PATTERNS[2][1]
    qkv_w = 3 * GROUP_WIDTH
    const = lambda bi, i: (0, 0)
    tab_spec = pl.BlockSpec((tm, LANES), lambda bi, i: (i, 0))
    return pl.pallas_call(
        _in_proj_kernel,
        grid=(b, s // tm),
        in_specs=[
            pl.BlockSpec((1, tm, dm), lambda bi, i: (bi, i, 0)),
            pl.BlockSpec((1, dm), const),
            pl.BlockSpec(w_in.shape, const),
            tab_spec, tab_spec, tab_spec,
        ],
        out_specs=[
            pl.BlockSpec((1, tm, POOL_WIDTH), lambda bi, i: (bi, i, 0)),
            pl.BlockSpec((1, tm, qkv_w), lambda bi, i: (bi, i, 0)),
            pl.BlockSpec((1, d2, tm // d2, qkv_w), lambda bi, i: (bi, 0, i, 0)),
            pl.BlockSpec((1, d3, tm // d3, qkv_w), lambda bi, i: (bi, 0, i, 0)),
        ],
        out_shape=[
            jax.ShapeDtypeStruct((b, s, POOL_WIDTH), F32),
            jax.ShapeDtypeStruct((b, s, qkv_w), BF16),
            jax.ShapeDtypeStruct((b, d2, s // d2, qkv_w), BF16),
            jax.ShapeDtypeStruct((b, d3, s // d3, qkv_w), BF16),
        ],
        scratch_shapes=[pltpu.VMEM((2 * qkv_w // LANES, tm, LANES), F32)],
        compiler_params=pltpu.CompilerParams(
            dimension_semantics=("parallel", "parallel"), vmem_limit_bytes=VMEM_LIMIT_BYTES),
        name="in_proj",
    )(x, norm_g, w_in, *rope_tabs)


def _attend_pairs(q, k, v, mask):
    lane = lax.broadcasted_iota(jnp.int32, (Q_BLOCK, LANES), 1)
    low = lane < HEAD_DIM
    outs, lses = [], []
    for j in range(PAIRS):
        sl = slice(j * LANES, (j + 1) * LANES)
        q2, k2, v2 = q[:, sl], k[:, sl], v[:, sl]
        o_pair, l_pair = [], []
        for sel in (low, jnp.logical_not(low)):
            qm = jnp.where(sel, q2, jnp.zeros_like(q2))
            s = lax.dot_general(qm, k2, (((1,), (1,)), ((), ())), preferred_element_type=F32)
            s = jnp.where(mask, s, -jnp.inf)
            m = jnp.max(s, axis=-1, keepdims=True)
            p = jnp.exp(s - m)
            den = jnp.sum(p, axis=-1, keepdims=True)
            o = jnp.dot(p.astype(BF16), v2, preferred_element_type=F32) / den
            o_pair.append(o)
            l_pair.append(m + jnp.log(den))
        outs.append(jnp.where(low, o_pair[0], o_pair[1]))
        lses.append(jnp.where(low, l_pair[0], l_pair[1]))
    return outs, lses


def _attn_kernel(qkv1_ref, qkv2_ref, qkv3_ref, out_ref, o_ref, l_ref):
    seq = out_ref.shape[1]
    gw = GROUP_WIDTH
    rel = (lax.broadcasted_iota(jnp.int32, (Q_BLOCK, KEY_SPAN), 1)
           - lax.broadcasted_iota(jnp.int32, (Q_BLOCK, KEY_SPAN), 0))

    def banded(g, load, n, row_of):
        def body(i, carry):
            qs = pl.multiple_of(i * Q_BLOCK, Q_BLOCK)
            ws = pl.multiple_of(jnp.clip(qs - BAND, 0, n - KEY_SPAN), BAND)
            mask = jnp.abs(rel + (ws - qs)) <= BAND
            q = load(qs, Q_BLOCK, 0)
            k = load(ws, KEY_SPAN, gw)
            v = load(ws, KEY_SPAN, 2 * gw)
            o, l = _attend_pairs(q, k, v, mask)
            rows = row_of(qs)
            for j in range(PAIRS):
                o_ref[g, j, rows, :] = o[j]
                l_ref[g, j, rows, :] = l[j]
            return carry

        lax.fori_loop(0, n // Q_BLOCK, body, 0)

    banded(0, lambda start, size, col: qkv1_ref[0, pl.ds(start, size), col:col + gw], seq,
           lambda qs: pl.ds(qs, Q_BLOCK))

    d2 = ATTN_PATTERNS[1][1]

    def per_residue(r, carry):
        banded(1, lambda start, size, col: qkv2_ref[0, r, pl.ds(start, size), col:col + gw], seq // d2,
               lambda qs: pl.ds(qs * d2 + r, Q_BLOCK, stride=d2))
        return carry

    lax.fori_loop(0, d2, per_residue, 0)

    d3 = ATTN_PATTERNS[2][1]
    n3 = seq // d3
    rel3 = rel[:, :n3]
    mask3 = jnp.abs(rel3) <= BAND

    def per_residue3(r, carry):
        q = qkv3_ref[0, r, :, 0:gw]
        k = qkv3_ref[0, r, :, gw:2 * gw]
        v = qkv3_ref[0, r, :, 2 * gw:3 * gw]
        o, l = _attend_pairs(q, k, v, mask3)
        rows = pl.ds(r, n3, stride=d3)
        for j in range(PAIRS):
            o_ref[2, j, rows, :] = o[j]
            l_ref[2, j, rows, :] = l[j]
        return carry

    lax.fori_loop(0, d3, per_residue3, 0)

    def merge(i, carry):
        rows = pl.ds(pl.multiple_of(i * Q_BLOCK, Q_BLOCK), Q_BLOCK)
        for j in range(PAIRS):
            l0, l1, l2 = l_ref[0, j, rows, :], l_ref[1, j, rows, :], l_ref[2, j, rows, :]
            top = jnp.maximum(jnp.maximum(l0, l1), l2)
            e0, e1, e2 = jnp.exp(l0 - top), jnp.exp(l1 - top), jnp.exp(l2 - top)
            num = e0 * o_ref[0, j, rows, :] + e1 * o_ref[1, j, rows, :] + e2 * o_ref[2, j, rows, :]
            out_ref[0, rows, j * LANES:(j + 1) * LANES] = (num / (e0 + e1 + e2)).astype(out_ref.dtype)
        return carry

    lax.fori_loop(0, seq // Q_BLOCK, merge, 0)


def _attention(qkv1, qkv2, qkv3):
    b, s, _ = qkv1.shape
    assert s // ATTN_PATTERNS[2][1] == Q_BLOCK and (s // ATTN_PATTERNS[1][1]) % Q_BLOCK == 0
    assert all(w // 2 // d == BAND for w, d in ATTN_PATTERNS)
    return pl.pallas_call(
        _attn_kernel,
        grid=(b,),
        in_specs=[
            pl.BlockSpec((1,) + qkv1.shape[1:], lambda bi: (bi, 0, 0)),
            pl.BlockSpec((1,) + qkv2.shape[1:], lambda bi: (bi, 0, 0, 0)),
            pl.BlockSpec((1,) + qkv3.shape[1:], lambda bi: (bi, 0, 0, 0)),
        ],
        out_specs=pl.BlockSpec((1, s, GROUP_WIDTH), lambda bi: (bi, 0, 0)),
        out_shape=jax.ShapeDtypeStruct((b, s, GROUP_WIDTH), BF16),
        scratch_shapes=[pltpu.VMEM((N_GROUPS, PAIRS, s, LANES), F32)] * 2,
        compiler_params=pltpu.CompilerParams(
            dimension_semantics=("parallel",), vmem_limit_bytes=VMEM_LIMIT_BYTES),
        name="attention",
    )(qkv1, qkv2, qkv3)


def _mix_kernel(x_ref, u_ref, up_ref, un_ref, at_ref, g_ref, maps_ref, ps_ref, wup_ref, wua_ref, wg_ref, bg_ref,
                wo_ref, out_ref, ext_ref):
    tm = x_ref.shape[1]
    dm = x_ref.shape[2]
    i = pl.program_id(1)
    seq = tm * pl.num_programs(1)
    x = x_ref[0]
    h = _rms_scale(x, g_ref[...]).astype(BF16)

    ext_ref[0:POOL_HALO, :] = jnp.where(i > 0, up_ref[0], 0.0)
    ext_ref[POOL_HALO:POOL_HALO + tm, :] = u_ref[0]
    ext_ref[POOL_HALO + tm:, :] = jnp.where(i < pl.num_programs(1) - 1, un_ref[0], 0.0)

    pos = i * tm + lax.broadcasted_iota(jnp.int32, (tm, 1), 0)
    mixed = []
    for g, w in enumerate(POOL_WINDOWS):
        cols = slice(g * POOL_GROUP, (g + 1) * POOL_GROUP)
        total = ext_ref[POOL_HALO - w // 2:POOL_HALO - w // 2 + tm, cols]
        for k in range(1 - w // 2, w // 2):
            total = total + ext_ref[POOL_HALO + k:POOL_HALO + k + tm, cols]
        cnt = (jnp.minimum(pos + w // 2, seq) - jnp.maximum(pos - w // 2, 0)).astype(F32)
        pooled = total / cnt - ext_ref[POOL_HALO:POOL_HALO + tm, cols]
        mg = jnp.dot(pooled.astype(BF16), maps_ref[g], preferred_element_type=F32)
        mixed.append((mg * ps_ref[:, cols]).astype(BF16))
    mixed = jnp.concatenate(mixed, axis=-1)

    a = jnp.dot(mixed, wup_ref[...], preferred_element_type=F32)
    b = jnp.dot(at_ref[0], wua_ref[...], preferred_element_type=F32)
    gates = jax.nn.sigmoid(jnp.dot(h, wg_ref[...], preferred_element_type=F32) + bg_ref[...])
    y = (gates[:, :dm] * a + gates[:, dm:] * b).astype(BF16)
    out_ref[0] = x + jnp.dot(y, wo_ref[...], preferred_element_type=F32)


def _mix(x, u, attn, norm_g, maps, pool_scale, w_up_pool, w_up_attn, w_gate, b_gate, w_out):
    b, s, dm = x.shape
    tm = TOKEN_TILE
    halo_blocks = tm // POOL_HALO
    last_halo = s // POOL_HALO - 1
    tile = lambda bi, i: (bi, i, 0)
    const2 = lambda bi, i: (0, 0)
    const3 = lambda bi, i: (0, 0, 0)
    return pl.pallas_call(
        _mix_kernel,
        grid=(b, s // tm),
        in_specs=[
            pl.BlockSpec((1, tm, dm), tile),
            pl.BlockSpec((1, tm, POOL_WIDTH), tile),
            pl.BlockSpec((1, POOL_HALO, POOL_WIDTH), lambda bi, i: (bi, jnp.maximum(i * halo_blocks - 1, 0), 0)),
            pl.BlockSpec((1, POOL_HALO, POOL_WIDTH),
                         lambda bi, i: (bi, jnp.minimum((i + 1) * halo_blocks, last_halo), 0)),
            pl.BlockSpec((1, tm, GROUP_WIDTH), tile),
            pl.BlockSpec((1, dm), const2),
            pl.BlockSpec(maps.shape, const3),
            pl.BlockSpec((1, POOL_WIDTH), const2),
            pl.BlockSpec(w_up_pool.shape, const2),
            pl.BlockSpec(w_up_attn.shape, const2),
            pl.BlockSpec(w_gate.shape, const2),
            pl.BlockSpec((1, 2 * dm), const2),
            pl.BlockSpec(w_out.shape, const2),
        ],
        out_specs=pl.BlockSpec((1, tm, dm), tile),
        out_shape=jax.ShapeDtypeStruct((b, s, dm), F32),
        scratch_shapes=[pltpu.VMEM((tm + 2 * POOL_HALO, POOL_WIDTH), F32)],
        compiler_params=pltpu.CompilerParams(
            dimension_semantics=("parallel", "parallel"), vmem_limit_bytes=VMEM_LIMIT_BYTES),
        name="mix",
    )(x, u, u, u, attn, norm_g, maps, pool_scale, w_up_pool, w_up_attn, w_gate, b_gate, w_out)


def _ffn_kernel(x_ref, g_ref, wg_ref, wu_ref, wd_ref, gf_ref, out_ref):
    x = x_ref[...]
    h = _rms_scale(x, g_ref[...]).astype(BF16)
    gate = jnp.dot(h, wg_ref[...], preferred_element_type=F32)
    up = jnp.dot(h, wu_ref[...], preferred_element_type=F32)
    act = (gate * jax.nn.sigmoid(gate) * up).astype(BF16)
    x2 = x + jnp.dot(act, wd_ref[...], preferred_element_type=F32)
    out_ref[...] = _rms_scale(x2, gf_ref[...])


def _ffn(x, norm_g, w_gate, w_up, w_down, norm_final):
    n, dm = x.shape
    tm = TOKEN_TILE
    const = lambda i: (0, 0)
    resident = functools.partial(pl.BlockSpec, index_map=const, pipeline_mode=pl.Buffered(1))
    return pl.pallas_call(
        _ffn_kernel,
        grid=(n // tm,),
        in_specs=[
            pl.BlockSpec((tm, dm), lambda i: (i, 0)),
            pl.BlockSpec((1, dm), const),
            resident(w_gate.shape),
            resident(w_up.shape),
            resident(w_down.shape),
            pl.BlockSpec((1, dm), const),
        ],
        out_specs=pl.BlockSpec((tm, dm), lambda i: (i, 0)),
        out_shape=jax.ShapeDtypeStruct((n, dm), F32),
        compiler_params=pltpu.CompilerParams(
            dimension_semantics=("parallel",), vmem_limit_bytes=VMEM_LIMIT_BYTES),
        name="ffn",
    )(x, norm_g, w_gate, w_up, w_down, norm_final)


def _encoder(x, p, rope_tabs):
    b, s, dm = x.shape
    u, qkv1, qkv2, qkv3 = _in_proj(x, p["norm_mix"], p["w_in"], rope_tabs)
    attn = _attention(qkv1, qkv2, qkv3)
    x1 = _mix(x, u, attn, p["norm_mix"], p["pool_maps"], p["pool_scale"], p["w_up_pool"], p["w_up_attn"],
              p["w_gate"], p["b_gate"], p["w_out"])
    y = _ffn(x1.reshape(b * s, dm), p["norm_ffn"], p["w_ffn_gate"], p["w_ffn_up"], p["w_ffn_down"], p["norm_final"])
    return y.reshape(b, s, dm)


def kernel(x_prompt, x_sample, norm_mix, w_in, pool_maps, pool_scale, w_up_pool, w_up_attn, w_gate, b_gate,
           w_out, norm_ffn, w_ffn_gate, w_ffn_up, w_ffn_down, norm_final):
    assert norm_mix.shape[0] == 1, "one encoder layer"
    p = {
        "norm_mix": norm_mix[0][None, :],
        "w_in": w_in[0].astype(BF16),
        "pool_maps": pool_maps[0].astype(BF16),
        "pool_scale": pool_scale[0][None, :],
        "w_up_pool": w_up_pool[0].astype(BF16),
        "w_up_attn": w_up_attn[0].astype(BF16),
        "w_gate": w_gate[0].astype(BF16),
        "b_gate": b_gate[0][None, :],
        "w_out": w_out[0].astype(BF16),
        "norm_ffn": norm_ffn[0][None, :],
        "w_ffn_gate": w_ffn_gate[0].astype(BF16),
        "w_ffn_up": w_ffn_up[0].astype(BF16),
        "w_ffn_down": w_ffn_down[0].astype(BF16),
        "norm_final": norm_final[None, :],
    }
    outs = []
    for x in (x_prompt, x_sample):
        rope_tabs = _rope_tables(x.shape[1])
        outs.append(_encoder(x, p, rope_tabs))
    return tuple(outs)
```

```python
import functools

import jax
import jax.numpy as jnp
import numpy as np
from jax import lax
from jax.experimental import pallas as pl
from jax.experimental.pallas import tpu as pltpu

HEAD_DIM = 64
HEADS_PER_GROUP = 4
ATTN_PATTERNS = ((128, 1), (512, 4), (2048, 16))
N_GROUPS = len(ATTN_PATTERNS)
GROUP_WIDTH = HEADS_PER_GROUP * HEAD_DIM
ATTN_WIDTH = N_GROUPS * GROUP_WIDTH
ROPE_DIM = HEAD_DIM // 4
ROPE_THETA = 500000.0
POOL_WINDOWS = (2, 4, 8, 16)
POOL_GROUP = 128
POOL_WIDTH = POOL_GROUP * len(POOL_WINDOWS)
POOL_HALO = 8
NORM_EPS = 1e-6

LANES = 128
Q_BLOCK = 128
BAND = 64
KEY_SPAN = Q_BLOCK + 2 * BAND
PAIRS = GROUP_WIDTH // LANES
UNITS = 4
TOKEN_TILE = 512
VMEM_LIMIT_BYTES = 56 * 1024 * 1024

F32 = jnp.float32
BF16 = jnp.bfloat16


def _rms_scale(x, gain):
    return x * lax.rsqrt(jnp.mean(x * x, axis=-1, keepdims=True) + NORM_EPS) * gain


def _rope_tables(seq):
    half = ROPE_DIM // 2
    inv = (ROPE_THETA ** (-np.arange(half, dtype=np.float32) / half)).astype(np.float32).astype(np.float64)
    ang = np.arange(seq, dtype=np.float64)[:, None] * inv[None, :]
    cos, sin = np.cos(ang), np.sin(ang)
    c = np.ones((seq, HEAD_DIM))
    s_up = np.zeros((seq, HEAD_DIM))
    s_down = np.zeros((seq, HEAD_DIM))
    c[:, :half] = cos
    c[:, half:ROPE_DIM] = cos
    s_down[:, :half] = -sin
    s_up[:, half:ROPE_DIM] = sin
    rep = LANES // HEAD_DIM
    return tuple(jnp.asarray(np.tile(t, (1, rep)), dtype=F32) for t in (c, s_up, s_down))


def _in_proj_kernel(x_ref, g_ref, w_ref, c_ref, su_ref, sd_ref, u_ref, qkv1_ref, qkv2_ref, qkv3_ref, perm_ref):
    tm = x_ref.shape[1]
    h = _rms_scale(x_ref[0], g_ref[...]).astype(BF16)
    z = jnp.dot(h, w_ref[...], preferred_element_type=F32)
    u_ref[0] = z[:, :POOL_WIDTH]

    c, su, sd = c_ref[...], su_ref[...], sd_ref[...]
    half = ROPE_DIM // 2

    def rope(t):
        return t * c + pltpu.roll(t, half, 1) * su + pltpu.roll(t, LANES - half, 1) * sd

    scale = HEAD_DIM ** -0.5
    chunks = 3 * GROUP_WIDTH // LANES
    for g in range(N_GROUPS):
        for part in range(3):
            for j in range(GROUP_WIDTH // LANES):
                src = POOL_WIDTH + part * ATTN_WIDTH + g * GROUP_WIDTH + j * LANES
                t = z[:, src:src + LANES]
                if part < 2:
                    t = rope(t)
                if part == 0:
                    t = t * scale
                dst = part * GROUP_WIDTH + j * LANES
                if g == 0:
                    qkv1_ref[0, :, dst:dst + LANES] = t.astype(BF16)
                else:
                    perm_ref[(g - 1) * chunks + dst // LANES] = t

    for g, out_ref in ((1, qkv2_ref), (2, qkv3_ref)):
        d = ATTN_PATTERNS[g][1]
        for r in range(d):
            for ch in range(chunks):
                rows = perm_ref[(g - 1) * chunks + ch, pl.ds(r, tm // d, stride=d), :]
                out_ref[0, r, :, ch * LANES:(ch + 1) * LANES] = rows.astype(BF16)


def _in_proj(x, norm_g, w_in, rope_tabs):
    b, s, dm = x.shape
    tm = TOKEN_TILE
    d2, d3 = ATTN_PATTERNS[1][1], ATTN_PATTERNS[2][1]
    qkv_w = 3 * GROUP_WIDTH
    const = lambda bi, i: (0, 0)
    tab_spec = pl.BlockSpec((tm, LANES), lambda bi, i: (i, 0))
    return pl.pallas_call(
        _in_proj_kernel,
        grid=(b, s // tm),
        in_specs=[
            pl.BlockSpec((1, tm, dm), lambda bi, i: (bi, i, 0)),
            pl.BlockSpec((1, dm), const),
            pl.BlockSpec(w_in.shape, const),
            tab_spec, tab_spec, tab_spec,
        ],
        out_specs=[
            pl.BlockSpec((1, tm, POOL_WIDTH), lambda bi, i: (bi, i, 0)),
            pl.BlockSpec((1, tm, qkv_w), lambda bi, i: (bi, i, 0)),
            pl.BlockSpec((1, d2, tm // d2, qkv_w), lambda bi, i: (bi, 0, i, 0)),
            pl.BlockSpec((1, d3, tm // d3, qkv_w), lambda bi, i: (bi, 0, i, 0)),
        ],
        out_shape=[
            jax.ShapeDtypeStruct((b, s, POOL_WIDTH), F32),
            jax.ShapeDtypeStruct((b, s, qkv_w), BF16),
            jax.ShapeDtypeStruct((b, d2, s // d2, qkv_w), BF16),
            jax.ShapeDtypeStruct((b, d3, s // d3, qkv_w), BF16),
        ],
        scratch_shapes=[pltpu.VMEM((2 * qkv_w // LANES, tm, LANES), F32)],
        compiler_params=pltpu.CompilerParams(
            dimension_semantics=("parallel", "parallel"), vmem_limit_bytes=VMEM_LIMIT_BYTES),
        name="in_proj",
    )(x, norm_g, w_in, *rope_tabs)


def _attend_chunk(g, units, nk, s_ref, p_ref, dn_ref, o_ref, l_ref):
    low = lax.broadcasted_iota(jnp.int32, (Q_BLOCK, LANES), 1) < HEAD_DIM
    for u, (load, mask, rows) in enumerate(units):
        for j in range(PAIRS):
            q2 = load(0, j)
            zero = jnp.zeros_like(q2)
            stacked = jnp.concatenate([jnp.where(low, q2, zero), jnp.where(low, zero, q2)], axis=0)
            s_ref[u, j, :, :nk] = lax.dot_general(stacked, load(1, j), (((1,), (1,)), ((), ())),
                                                  preferred_element_type=F32)
    for u, (load, mask, rows) in enumerate(units):
        for j in range(PAIRS):
            lse = []
            for head in range(2):
                hrows = slice(head * Q_BLOCK, (head + 1) * Q_BLOCK)
                s = jnp.where(mask, s_ref[u, j, hrows, :nk], -jnp.inf)
                m = jnp.max(s, axis=-1, keepdims=True)
                p = jnp.exp(s - m)
                den = jnp.sum(p, axis=-1, keepdims=True)
                p_ref[u, j, hrows, :nk] = p.astype(BF16)
                dn_ref[u, j, hrows, :] = jnp.broadcast_to(1.0 / den, (Q_BLOCK, LANES))
                lse.append(m + jnp.log(den))
            l_ref[g, j, rows, :] = jnp.where(low, lse[0], lse[1])
    for u, (load, mask, rows) in enumerate(units):
        for j in range(PAIRS):
            o2 = jnp.dot(p_ref[u, j, :, :nk], load(2, j), preferred_element_type=F32) * dn_ref[u, j]
            o_ref[g, j, rows, :] = jnp.where(low, o2[:Q_BLOCK], o2[Q_BLOCK:])


def _attn_kernel(qkv1_ref, qkv2_ref, qkv3_ref, out_ref, o_ref, l_ref, s_ref, p_ref, dn_ref):
    seq = out_ref.shape[1]
    gw = GROUP_WIDTH
    scratch = (s_ref, p_ref, dn_ref, o_ref, l_ref)
    rel = (lax.broadcasted_iota(jnp.int32, (Q_BLOCK, KEY_SPAN), 1)
           - lax.broadcasted_iota(jnp.int32, (Q_BLOCK, KEY_SPAN), 0))

    def window(qs, n):
        ws = jnp.clip(qs - BAND, 0, n - KEY_SPAN)
        return ws, jnp.abs(rel + (ws - qs)) <= BAND

    def cols(part, j):
        return slice(part * gw + j * LANES, part * gw + (j + 1) * LANES)

    def chunk0(c, carry):
        units = []
        for b in range(UNITS):
            qs = pl.multiple_of((c * UNITS + b) * Q_BLOCK, Q_BLOCK)
            ws, mask = window(qs, seq)
            ws = pl.multiple_of(ws, BAND)

            def load(part, j, qs=qs, ws=ws):
                rows = pl.ds(qs, Q_BLOCK) if part == 0 else pl.ds(ws, KEY_SPAN)
                return qkv1_ref[0, rows, cols(part, j)]

            units.append((load, mask, pl.ds(qs, Q_BLOCK)))
        _attend_chunk(0, units, KEY_SPAN, *scratch)
        return carry

    lax.fori_loop(0, seq // (UNITS * Q_BLOCK), chunk0, 0)

    d2 = ATTN_PATTERNS[1][1]
    n2 = seq // d2

    def chunk1(r, carry):
        units = []
        for b in range(UNITS):
            qs = b * Q_BLOCK
            ws, mask = window(qs, n2)

            def load(part, j, qs=qs, ws=ws):
                rows = pl.ds(qs, Q_BLOCK) if part == 0 else pl.ds(ws, KEY_SPAN)
                return qkv2_ref[0, r, rows, cols(part, j)]

            units.append((load, mask, pl.ds(qs * d2 + r, Q_BLOCK, stride=d2)))
        _attend_chunk(1, units, KEY_SPAN, *scratch)
        return carry

    lax.fori_loop(0, d2, chunk1, 0)

    d3 = ATTN_PATTERNS[2][1]
    n3 = seq // d3
    mask3 = jnp.abs(rel[:, :n3]) <= BAND

    def chunk2(c, carry):
        units = []
        for b in range(UNITS):
            r = c * UNITS + b
            units.append((lambda part, j, r=r: qkv3_ref[0, r, :, cols(part, j)], mask3, pl.ds(r, n3, stride=d3)))
        _attend_chunk(2, units, n3, *scratch)
        return carry

    lax.fori_loop(0, d3 // UNITS, chunk2, 0)

    def merge(i, carry):
        rows = pl.ds(pl.multiple_of(i * Q_BLOCK, Q_BLOCK), Q_BLOCK)
        for j in range(PAIRS):
            l0, l1, l2 = l_ref[0, j, rows, :], l_ref[1, j, rows, :], l_ref[2, j, rows, :]
            top = jnp.maximum(jnp.maximum(l0, l1), l2)
            e0, e1, e2 = jnp.exp(l0 - top), jnp.exp(l1 - top), jnp.exp(l2 - top)
            num = e0 * o_ref[0, j, rows, :] + e1 * o_ref[1, j, rows, :] + e2 * o_ref[2, j, rows, :]
            out_ref[0, rows, j * LANES:(j + 1) * LANES] = (num / (e0 + e1 + e2)).astype(out_ref.dtype)
        return carry

    lax.fori_loop(0, seq // Q_BLOCK, merge, 0)


def _attention(qkv1, qkv2, qkv3):
    b, s, _ = qkv1.shape
    assert s // ATTN_PATTERNS[2][1] == Q_BLOCK and s // ATTN_PATTERNS[1][1] == UNITS * Q_BLOCK
    assert all(w // 2 // d == BAND for w, d in ATTN_PATTERNS)
    return pl.pallas_call(
        _attn_kernel,
        grid=(b,),
        in_specs=[
            pl.BlockSpec((1,) + qkv1.shape[1:], lambda bi: (bi, 0, 0)),
            pl.BlockSpec((1,) + qkv2.shape[1:], lambda bi: (bi, 0, 0, 0)),
            pl.BlockSpec((1,) + qkv3.shape[1:], lambda bi: (bi, 0, 0, 0)),
        ],
        out_specs=pl.BlockSpec((1, s, GROUP_WIDTH), lambda bi: (bi, 0, 0)),
        out_shape=jax.ShapeDtypeStruct((b, s, GROUP_WIDTH), BF16),
        scratch_shapes=[
            pltpu.VMEM((N_GROUPS, PAIRS, s, LANES), F32),
            pltpu.VMEM((N_GROUPS, PAIRS, s, LANES), F32),
            pltpu.VMEM((UNITS, PAIRS, 2 * Q_BLOCK, KEY_SPAN), F32),
            pltpu.VMEM((UNITS, PAIRS, 2 * Q_BLOCK, KEY_SPAN), BF16),
            pltpu.VMEM((UNITS, PAIRS, 2 * Q_BLOCK, LANES), F32),
        ],
        compiler_params=pltpu.CompilerParams(
            dimension_semantics=("parallel",), vmem_limit_bytes=VMEM_LIMIT_BYTES),
        name="attention",
    )(qkv1, qkv2, qkv3)


def _mix_kernel(x_ref, u_ref, up_ref, un_ref, at_ref, g_ref, maps_ref, ps_ref, wup_ref, wua_ref, wg_ref, bg_ref,
                wo_ref, out_ref, ext_ref):
    tm = x_ref.shape[1]
    dm = x_ref.shape[2]
    i = pl.program_id(1)
    seq = tm * pl.num_programs(1)
    x = x_ref[0]
    h = _rms_scale(x, g_ref[...]).astype(BF16)

    ext_ref[0:POOL_HALO, :] = jnp.where(i > 0, up_ref[0], 0.0)
    ext_ref[POOL_HALO:POOL_HALO + tm, :] = u_ref[0]
    ext_ref[POOL_HALO + tm:, :] = jnp.where(i < pl.num_programs(1) - 1, un_ref[0], 0.0)

    pos = i * tm + lax.broadcasted_iota(jnp.int32, (tm, 1), 0)
    mixed = []
    for g, w in enumerate(POOL_WINDOWS):
        cols = slice(g * POOL_GROUP, (g + 1) * POOL_GROUP)
        total = ext_ref[POOL_HALO - w // 2:POOL_HALO - w // 2 + tm, cols]
        for k in range(1 - w // 2, w // 2):
            total = total + ext_ref[POOL_HALO + k:POOL_HALO + k + tm, cols]
        cnt = (jnp.minimum(pos + w // 2, seq) - jnp.maximum(pos - w // 2, 0)).astype(F32)
        pooled = total / cnt - ext_ref[POOL_HALO:POOL_HALO + tm, cols]
        mg = jnp.dot(pooled.astype(BF16), maps_ref[g], preferred_element_type=F32)
        mixed.append((mg * ps_ref[:, cols]).astype(BF16))
    mixed = jnp.concatenate(mixed, axis=-1)

    a = jnp.dot(mixed, wup_ref[...], preferred_element_type=F32)
    b = jnp.dot(at_ref[0], wua_ref[...], preferred_element_type=F32)
    gates = jax.nn.sigmoid(jnp.dot(h, wg_ref[...], preferred_element_type=F32) + bg_ref[...])
    y = (gates[:, :dm] * a + gates[:, dm:] * b).astype(BF16)
    out_ref[0] = x + jnp.dot(y, wo_ref[...], preferred_element_type=F32)


def _mix(x, u, attn, norm_g, maps, pool_scale, w_up_pool, w_up_attn, w_gate, b_gate, w_out):
    b, s, dm = x.shape
    tm = TOKEN_TILE
    halo_blocks = tm // POOL_HALO
    last_halo = s // POOL_HALO - 1
    tile = lambda bi, i: (bi, i, 0)
    const2 = lambda bi, i: (0, 0)
    const3 = lambda bi, i: (0, 0, 0)
    return pl.pallas_call(
        _mix_kernel,
        grid=(b, s // tm),
        in_specs=[
            pl.BlockSpec((1, tm, dm), tile),
            pl.BlockSpec((1, tm, POOL_WIDTH), tile),
            pl.BlockSpec((1, POOL_HALO, POOL_WIDTH), lambda bi, i: (bi, jnp.maximum(i * halo_blocks - 1, 0), 0)),
            pl.BlockSpec((1, POOL_HALO, POOL_WIDTH),
                         lambda bi, i: (bi, jnp.minimum((i + 1) * halo_blocks, last_halo), 0)),
            pl.BlockSpec((1, tm, GROUP_WIDTH), tile),
            pl.BlockSpec((1, dm), const2),
            pl.BlockSpec(maps.shape, const3),
            pl.BlockSpec((1, POOL_WIDTH), const2),
            pl.BlockSpec(w_up_pool.shape, const2),
            pl.BlockSpec(w_up_attn.shape, const2),
            pl.BlockSpec(w_gate.shape, const2),
            pl.BlockSpec((1, 2 * dm), const2),
            pl.BlockSpec(w_out.shape, const2),
        ],
        out_specs=pl.BlockSpec((1, tm, dm), tile),
        out_shape=jax.ShapeDtypeStruct((b, s, dm), F32),
        scratch_shapes=[pltpu.VMEM((tm + 2 * POOL_HALO, POOL_WIDTH), F32)],
        compiler_params=pltpu.CompilerParams(
            dimension_semantics=("parallel", "parallel"), vmem_limit_bytes=VMEM_LIMIT_BYTES),
        name="mix",
    )(x, u, u, u, attn, norm_g, maps, pool_scale, w_up_pool, w_up_attn, w_gate, b_gate, w_out)


def _ffn_kernel(x_ref, g_ref, wg_ref, wu_ref, wd_ref, gf_ref, out_ref):
    x = x_ref[...]
    h = _rms_scale(x, g_ref[...]).astype(BF16)
    gate = jnp.dot(h, wg_ref[...], preferred_element_type=F32)
    up = jnp.dot(h, wu_ref[...], preferred_element_type=F32)
    act = (gate * jax.nn.sigmoid(gate) * up).astype(BF16)
    x2 = x + jnp.dot(act, wd_ref[...], preferred_element_type=F32)
    out_ref[...] = _rms_scale(x2, gf_ref[...])


def _ffn(x, norm_g, w_gate, w_up, w_down, norm_final):
    n, dm = x.shape
    tm = TOKEN_TILE
    const = lambda i: (0, 0)
    resident = functools.partial(pl.BlockSpec, index_map=const, pipeline_mode=pl.Buffered(1))
    return pl.pallas_call(
        _ffn_kernel,
        grid=(n // tm,),
        in_specs=[
            pl.BlockSpec((tm, dm), lambda i: (i, 0)),
            pl.BlockSpec((1, dm), const),
            resident(w_gate.shape),
            resident(w_up.shape),
            resident(w_down.shape),
            pl.BlockSpec((1, dm), const),
        ],
        out_specs=pl.BlockSpec((tm, dm), lambda i: (i, 0)),
        out_shape=jax.ShapeDtypeStruct((n, dm), F32),
        compiler_params=pltpu.CompilerParams(
            dimension_semantics=("parallel",), vmem_limit_bytes=VMEM_LIMIT_BYTES),
        name="ffn",
    )(x, norm_g, w_gate, w_up, w_down, norm_final)


def _encoder(x, p, rope_tabs):
    b, s, dm = x.shape
    u, qkv1, qkv2, qkv3 = _in_proj(x, p["norm_mix"], p["w_in"], rope_tabs)
    attn = _attention(qkv1, qkv2, qkv3)
    x1 = _mix(x, u, attn, p["norm_mix"], p["pool_maps"], p["pool_scale"], p["w_up_pool"], p["w_up_attn"],
              p["w_gate"], p["b_gate"], p["w_out"])
    y = _ffn(x1.reshape(b * s, dm), p["norm_ffn"], p["w_ffn_gate"], p["w_ffn_up"], p["w_ffn_down"], p["norm_final"])
    return y.reshape(b, s, dm)


def kernel(x_prompt, x_sample, norm_mix, w_in, pool_maps, pool_scale, w_up_pool, w_up_attn, w_gate, b_gate,
           w_out, norm_ffn, w_ffn_gate, w_ffn_up, w_ffn_down, norm_final):
    assert norm_mix.shape[0] == 1, "one encoder layer"
    p = {
        "norm_mix": norm_mix[0][None, :],
        "w_in": w_in[0].astype(BF16),
        "pool_maps": pool_maps[0].astype(BF16),
        "pool_scale": pool_scale[0][None, :],
        "w_up_pool": w_up_pool[0].astype(BF16),
        "w_up_attn": w_up_attn[0].astype(BF16),
        "w_gate": w_gate[0].astype(BF16),
        "b_gate": b_gate[0][None, :],
        "w_out": w_out[0].astype(BF16),
        "norm_ffn": norm_ffn[0][None, :],
        "w_ffn_gate": w_ffn_gate[0].astype(BF16),
        "w_ffn_up": w_ffn_up[0].astype(BF16),
        "w_ffn_down": w_ffn_down[0].astype(BF16),
        "norm_final": norm_final[None, :],
    }
    outs = []
    for x in (x_prompt, x_sample):
        rope_tabs = _rope_tables(x.shape[1])
        outs.append(_encoder(x, p, rope_tabs))
    return tuple(outs)
```

```python
import functools

import jax
import jax.numpy as jnp
import numpy as np
from jax import lax
from jax.experimental import pallas as pl
from jax.experimental.pallas import tpu as pltpu

HEAD_DIM = 64
HEADS_PER_GROUP = 4
ATTN_PATTERNS = ((128, 1), (512, 4), (2048, 16))
N_GROUPS = len(ATTN_PATTERNS)
GROUP_WIDTH = HEADS_PER_GROUP * HEAD_DIM
ATTN_WIDTH = N_GROUPS * GROUP_WIDTH
ROPE_DIM = HEAD_DIM // 4
ROPE_THETA = 500000.0
POOL_WINDOWS = (2, 4, 8, 16)
POOL_GROUP = 128
POOL_WIDTH = POOL_GROUP * len(POOL_WINDOWS)
POOL_HALO = 8
NORM_EPS = 1e-6

LANES = 128
Q_BLOCK = 128
BAND = 64
KEY_SPAN = Q_BLOCK + 2 * BAND
PAIRS = GROUP_WIDTH // LANES
UNITS = 4
TOKEN_TILE = 1024
IN_TILE = 1024
FFN_TILE = 1024
SUB_ROWS = 256
COL_CHUNK = 256
WIDE_CHUNK = 512
VMEM_LIMIT_BYTES = 56 * 1024 * 1024

F32 = jnp.float32
BF16 = jnp.bfloat16


def _rms_scale(x, gain):
    return x * lax.rsqrt(jnp.mean(x * x, axis=-1, keepdims=True) + NORM_EPS) * gain


def _rope_tables(seq):
    half = ROPE_DIM // 2
    inv = (ROPE_THETA ** (-np.arange(half, dtype=np.float32) / half)).astype(np.float32).astype(np.float64)
    ang = np.arange(seq, dtype=np.float64)[:, None] * inv[None, :]
    cos, sin = np.cos(ang), np.sin(ang)
    c = np.ones((seq, HEAD_DIM))
    s_up = np.zeros((seq, HEAD_DIM))
    s_down = np.zeros((seq, HEAD_DIM))
    c[:, :half] = cos
    c[:, half:ROPE_DIM] = cos
    s_down[:, :half] = -sin
    s_up[:, half:ROPE_DIM] = sin
    rep = LANES // HEAD_DIM
    return tuple(jnp.asarray(np.tile(t, (1, rep)), dtype=F32) for t in (c, s_up, s_down))


def _in_proj_kernel(x_ref, g_ref, w_ref, c_ref, su_ref, sd_ref, u_ref, qkv1_ref, qkv2_ref, qkv3_ref, perm_ref):
    tm = x_ref.shape[1]
    n_cols = w_ref.shape[1]
    assert POOL_WIDTH % GROUP_WIDTH == 0 and WIDE_CHUNK % GROUP_WIDTH == 0
    half = ROPE_DIM // 2
    scale = HEAD_DIM ** -0.5
    chunks = 3 * GROUP_WIDTH // LANES

    def normed(r0):
        return _rms_scale(x_ref[0, r0:r0 + SUB_ROWS, :], g_ref[...]).astype(BF16)

    def emit_wide(r0, col, z):
        for off in range(0, z.shape[1], GROUP_WIDTH):
            emit(r0, col + off, z[:, off:off + GROUP_WIDTH])

    def emit(r0, col, z):
        rows = slice(r0, r0 + SUB_ROWS)
        if col < POOL_WIDTH:
            u_ref[0, rows, col:col + GROUP_WIDTH] = z
            return
        part, g = divmod((col - POOL_WIDTH) // GROUP_WIDTH, N_GROUPS)
        for j in range(GROUP_WIDTH // LANES):
            t = z[:, j * LANES:(j + 1) * LANES]
            if part < 2:
                t = (t * c_ref[rows, :] + pltpu.roll(t, half, 1) * su_ref[rows, :]
                     + pltpu.roll(t, LANES - half, 1) * sd_ref[rows, :])
            if part == 0:
                t = t * scale
            dst = part * GROUP_WIDTH + j * LANES
            if g == 0:
                qkv1_ref[0, rows, dst:dst + LANES] = t.astype(BF16)
            else:
                perm_ref[(g - 1) * chunks + dst // LANES, rows, :] = t

    def deinterleave(r0):
        for g, out_ref in ((1, qkv2_ref), (2, qkv3_ref)):
            d = ATTN_PATTERNS[g][1]
            n = SUB_ROWS // d
            for r in range(d):
                for ch in range(chunks):
                    rows = perm_ref[(g - 1) * chunks + ch, pl.ds(r0 + r, n, stride=d), :]
                    out_ref[0, r, r0 // d:r0 // d + n, ch * LANES:(ch + 1) * LANES] = rows.astype(BF16)

    h = normed(0)
    for r0 in range(0, tm, SUB_ROWS):
        pending = None
        for col in range(0, n_cols, WIDE_CHUNK):
            z = jnp.dot(h, w_ref[:, col:min(col + WIDE_CHUNK, n_cols)], preferred_element_type=F32)
            if pending is not None:
                emit_wide(r0, *pending)
            elif r0 > 0:
                deinterleave(r0 - SUB_ROWS)
            pending = (col, z)
        emit_wide(r0, *pending)
        if r0 + SUB_ROWS < tm:
            h = normed(r0 + SUB_ROWS)
    deinterleave(tm - SUB_ROWS)


def _in_proj(x, norm_g, w_in, rope_tabs):
    b, s, dm = x.shape
    tm = IN_TILE
    d2, d3 = ATTN_PATTERNS[1][1], ATTN_PATTERNS[2][1]
    qkv_w = 3 * GROUP_WIDTH
    const = lambda bi, i: (0, 0)
    tab_spec = pl.BlockSpec((tm, LANES), lambda bi, i: (i, 0))
    return pl.pallas_call(
        _in_proj_kernel,
        grid=(b, s // tm),
        in_specs=[
            pl.BlockSpec((1, tm, dm), lambda bi, i: (bi, i, 0)),
            pl.BlockSpec((1, dm), const),
            pl.BlockSpec(w_in.shape, const),
            tab_spec, tab_spec, tab_spec,
        ],
        out_specs=[
            pl.BlockSpec((1, tm, POOL_WIDTH), lambda bi, i: (bi, i, 0)),
            pl.BlockSpec((1, tm, qkv_w), lambda bi, i: (bi, i, 0)),
            pl.BlockSpec((1, d2, tm // d2, qkv_w), lambda bi, i: (bi, 0, i, 0)),
            pl.BlockSpec((1, d3, tm // d3, qkv_w), lambda bi, i: (bi, 0, i, 0)),
        ],
        out_shape=[
            jax.ShapeDtypeStruct((b, s, POOL_WIDTH), F32),
            jax.ShapeDtypeStruct((b, s, qkv_w), BF16),
            jax.ShapeDtypeStruct((b, d2, s // d2, qkv_w), BF16),
            jax.ShapeDtypeStruct((b, d3, s // d3, qkv_w), BF16),
        ],
        scratch_shapes=[pltpu.VMEM((2 * qkv_w // LANES, tm, LANES), F32)],
        compiler_params=pltpu.CompilerParams(
            dimension_semantics=("parallel", "parallel"), vmem_limit_bytes=VMEM_LIMIT_BYTES),
        name="in_proj",
    )(x, norm_g, w_in, *rope_tabs)


def _attend_chunk(g, units, nk, s_ref, p_ref, dn_ref, o_ref, l_ref):
    low = lax.broadcasted_iota(jnp.int32, (Q_BLOCK, LANES), 1) < HEAD_DIM
    for u, (load, mask, rows) in enumerate(units):
        for j in range(PAIRS):
            q2 = load(0, j)
            zero = jnp.zeros_like(q2)
            stacked = jnp.concatenate([jnp.where(low, q2, zero), jnp.where(low, zero, q2)], axis=0)
            s_ref[u, j, :, :nk] = lax.dot_general(stacked, load(1, j), (((1,), (1,)), ((), ())),
                                                  preferred_element_type=F32)
    for u, (load, mask, rows) in enumerate(units):
        for j in range(PAIRS):
            lse = []
            for head in range(2):
                hrows = slice(head * Q_BLOCK, (head + 1) * Q_BLOCK)
                s = jnp.where(mask, s_ref[u, j, hrows, :nk], -jnp.inf)
                m = jnp.max(s, axis=-1, keepdims=True)
                p = jnp.exp(s - m)
                den = jnp.sum(p, axis=-1, keepdims=True)
                p_ref[u, j, hrows, :nk] = p.astype(BF16)
                dn_ref[u, j, hrows, :] = jnp.broadcast_to(1.0 / den, (Q_BLOCK, LANES))
                lse.append(m + jnp.log(den))
            l_ref[g, j, rows, :] = jnp.where(low, lse[0], lse[1])
    for u, (load, mask, rows) in enumerate(units):
        for j in range(PAIRS):
            o2 = jnp.dot(p_ref[u, j, :, :nk], load(2, j), preferred_element_type=F32) * dn_ref[u, j]
            o_ref[g, j, rows, :] = jnp.where(low, o2[:Q_BLOCK], o2[Q_BLOCK:])


def _attn_kernel(qkv1_ref, qkv2_ref, qkv3_ref, out_ref, o_ref, l_ref, s_ref, p_ref, dn_ref):
    seq = out_ref.shape[1]
    gw = GROUP_WIDTH
    scratch = (s_ref, p_ref, dn_ref, o_ref, l_ref)
    rel = (lax.broadcasted_iota(jnp.int32, (Q_BLOCK, KEY_SPAN), 1)
           - lax.broadcasted_iota(jnp.int32, (Q_BLOCK, KEY_SPAN), 0))

    def window(qs, n):
        ws = jnp.clip(qs - BAND, 0, n - KEY_SPAN)
        return ws, jnp.abs(rel + (ws - qs)) <= BAND

    def cols(part, j):
        return slice(part * gw + j * LANES, part * gw + (j + 1) * LANES)

    def chunk0(c, carry):
        units = []
        for b in range(UNITS):
            qs = pl.multiple_of((c * UNITS + b) * Q_BLOCK, Q_BLOCK)
            ws, mask = window(qs, seq)
            ws = pl.multiple_of(ws, BAND)

            def load(part, j, qs=qs, ws=ws):
                rows = pl.ds(qs, Q_BLOCK) if part == 0 else pl.ds(ws, KEY_SPAN)
                return qkv1_ref[0, rows, cols(part, j)]

            units.append((load, mask, pl.ds(qs, Q_BLOCK)))
        _attend_chunk(0, units, KEY_SPAN, *scratch)
        return carry

    lax.fori_loop(0, seq // (UNITS * Q_BLOCK), chunk0, 0)

    d2 = ATTN_PATTERNS[1][1]
    n2 = seq // d2

    def chunk1(r, carry):
        units = []
        for b in range(UNITS):
            qs = b * Q_BLOCK
            ws, mask = window(qs, n2)

            def load(part, j, qs=qs, ws=ws):
                rows = pl.ds(qs, Q_BLOCK) if part == 0 else pl.ds(ws, KEY_SPAN)
                return qkv2_ref[0, r, rows, cols(part, j)]

            units.append((load, mask, pl.ds(qs * d2 + r, Q_BLOCK, stride=d2)))
        _attend_chunk(1, units, KEY_SPAN, *scratch)
        return carry

    lax.fori_loop(0, d2, chunk1, 0)

    d3 = ATTN_PATTERNS[2][1]
    n3 = seq // d3
    mask3 = jnp.abs(rel[:, :n3]) <= BAND

    def chunk2(c, carry):
        units = []
        for b in range(UNITS):
            r = c * UNITS + b
            units.append((lambda part, j, r=r: qkv3_ref[0, r, :, cols(part, j)], mask3, pl.ds(r, n3, stride=d3)))
        _attend_chunk(2, units, n3, *scratch)
        return carry

    lax.fori_loop(0, d3 // UNITS, chunk2, 0)

    def merge(i, carry):
        rows = pl.ds(pl.multiple_of(i * Q_BLOCK, Q_BLOCK), Q_BLOCK)
        for j in range(PAIRS):
            l0, l1, l2 = l_ref[0, j, rows, :], l_ref[1, j, rows, :], l_ref[2, j, rows, :]
            top = jnp.maximum(jnp.maximum(l0, l1), l2)
            e0, e1, e2 = jnp.exp(l0 - top), jnp.exp(l1 - top), jnp.exp(l2 - top)
            num = e0 * o_ref[0, j, rows, :] + e1 * o_ref[1, j, rows, :] + e2 * o_ref[2, j, rows, :]
            out_ref[0, rows, j * LANES:(j + 1) * LANES] = (num / (e0 + e1 + e2)).astype(out_ref.dtype)
        return carry

    lax.fori_loop(0, seq // Q_BLOCK, merge, 0)


def _attention(qkv1, qkv2, qkv3):
    b, s, _ = qkv1.shape
    assert s // ATTN_PATTERNS[2][1] == Q_BLOCK and s // ATTN_PATTERNS[1][1] == UNITS * Q_BLOCK
    assert all(w // 2 // d == BAND for w, d in ATTN_PATTERNS)
    return pl.pallas_call(
        _attn_kernel,
        grid=(b,),
        in_specs=[
            pl.BlockSpec((1,) + qkv1.shape[1:], lambda bi: (bi, 0, 0)),
            pl.BlockSpec((1,) + qkv2.shape[1:], lambda bi: (bi, 0, 0, 0)),
            pl.BlockSpec((1,) + qkv3.shape[1:], lambda bi: (bi, 0, 0, 0)),
        ],
        out_specs=pl.BlockSpec((1, s, GROUP_WIDTH), lambda bi: (bi, 0, 0)),
        out_shape=jax.ShapeDtypeStruct((b, s, GROUP_WIDTH), BF16),
        scratch_shapes=[
            pltpu.VMEM((N_GROUPS, PAIRS, s, LANES), F32),
            pltpu.VMEM((N_GROUPS, PAIRS, s, LANES), F32),
            pltpu.VMEM((UNITS, PAIRS, 2 * Q_BLOCK, KEY_SPAN), F32),
            pltpu.VMEM((UNITS, PAIRS, 2 * Q_BLOCK, KEY_SPAN), BF16),
            pltpu.VMEM((UNITS, PAIRS, 2 * Q_BLOCK, LANES), F32),
        ],
        compiler_params=pltpu.CompilerParams(
            dimension_semantics=("parallel",), vmem_limit_bytes=VMEM_LIMIT_BYTES),
        name="attention",
    )(qkv1, qkv2, qkv3)


def _mix_kernel(x_ref, u_ref, up_ref, un_ref, at_ref, g_ref, maps_ref, ps_ref, wup_ref, wua_ref, wg_ref, bg_ref,
                wo_ref, out_ref, ext_ref):
    tm = x_ref.shape[1]
    dm = x_ref.shape[2]
    i = pl.program_id(1)
    seq = tm * pl.num_programs(1)

    ext_ref[0:POOL_HALO, :] = jnp.where(i > 0, up_ref[0], 0.0)
    ext_ref[POOL_HALO:POOL_HALO + tm, :] = u_ref[0]
    ext_ref[POOL_HALO + tm:, :] = jnp.where(i < pl.num_programs(1) - 1, un_ref[0], 0.0)

    n_chunks = dm // COL_CHUNK
    assert n_chunks == len(POOL_WINDOWS), "one pooling group per column chunk"

    def normed(r0):
        return _rms_scale(x_ref[0, r0:r0 + SUB_ROWS, :], g_ref[...]).astype(BF16)

    def pool_group(r0, j):
        w = POOL_WINDOWS[j]
        cols = slice(j * POOL_GROUP, (j + 1) * POOL_GROUP)
        pos = i * tm + r0 + lax.broadcasted_iota(jnp.int32, (SUB_ROWS, 1), 0)
        span = SUB_ROWS + 2 * POOL_HALO
        ext = ext_ref[r0:r0 + span, cols]
        shifted = lambda a, k: pltpu.roll(a, span - k, 0)
        run, width = ext, 1
        while 2 * width < w:
            run, width = run + shifted(run, width), 2 * width
        start = POOL_HALO - w // 2
        assert 2 * width == w and start + width == POOL_HALO
        lead = run if start == 0 else shifted(run, start)
        total = lead[:SUB_ROWS] + run[POOL_HALO:POOL_HALO + SUB_ROWS]
        cnt = (jnp.minimum(pos + w // 2, seq) - jnp.maximum(pos - w // 2, 0)).astype(F32)
        pooled = total / cnt - ext[POOL_HALO:POOL_HALO + SUB_ROWS]
        return pooled.astype(BF16)

    def map_group(pooled, j):
        mg = jnp.dot(pooled, maps_ref[j], preferred_element_type=F32)
        return (mg * ps_ref[:, j * POOL_GROUP:(j + 1) * POOL_GROUP]).astype(BF16)

    def project(h, mixed, attn, c):
        ca, cb = slice(c, c + COL_CHUNK), slice(dm + c, dm + c + COL_CHUNK)
        dot = functools.partial(jnp.dot, preferred_element_type=F32)
        return (dot(h, wg_ref[:, ca]) + bg_ref[:, ca], dot(h, wg_ref[:, cb]) + bg_ref[:, cb],
                dot(mixed, wup_ref[:, ca]), dot(attn, wua_ref[:, ca]))

    def gate(parts):
        ga, gb, a, b = parts
        return (jax.nn.sigmoid(ga) * a + jax.nn.sigmoid(gb) * b).astype(BF16)

    h = normed(0)
    mixed = jnp.concatenate([map_group(pool_group(0, j), j) for j in range(n_chunks)], axis=-1)
    for r0 in range(0, tm, SUB_ROWS):
        more = r0 + SUB_ROWS < tm
        attn = at_ref[0, r0:r0 + SUB_ROWS, :]
        ys, pending, pooled, mapped = [], None, [], []
        for j in range(n_chunks):
            parts = project(h, mixed, attn, j * COL_CHUNK)
            if pooled:
                mapped.append(map_group(pooled[-1], j - 1))
            if pending is not None:
                ys.append(gate(pending))
            pending = parts
            if more:
                pooled.append(pool_group(r0 + SUB_ROWS, j))
        ys.append(gate(pending))
        delta = jnp.dot(jnp.concatenate(ys, axis=-1), wo_ref[...], preferred_element_type=F32)
        if more:
            mapped.append(map_group(pooled[-1], n_chunks - 1))
            h = normed(r0 + SUB_ROWS)
            mixed = jnp.concatenate(mapped, axis=-1)
        out_ref[0, r0:r0 + SUB_ROWS, :] = x_ref[0, r0:r0 + SUB_ROWS, :] + delta


def _mix(x, u, attn, norm_g, maps, pool_scale, w_up_pool, w_up_attn, w_gate, b_gate, w_out):
    b, s, dm = x.shape
    tm = TOKEN_TILE
    halo_blocks = tm // POOL_HALO
    last_halo = s // POOL_HALO - 1
    tile = lambda bi, i: (bi, i, 0)
    const2 = lambda bi, i: (0, 0)
    const3 = lambda bi, i: (0, 0, 0)
    return pl.pallas_call(
        _mix_kernel,
        grid=(b, s // tm),
        in_specs=[
            pl.BlockSpec((1, tm, dm), tile),
            pl.BlockSpec((1, tm, POOL_WIDTH), tile),
            pl.BlockSpec((1, POOL_HALO, POOL_WIDTH), lambda bi, i: (bi, jnp.maximum(i * halo_blocks - 1, 0), 0)),
            pl.BlockSpec((1, POOL_HALO, POOL_WIDTH),
                         lambda bi, i: (bi, jnp.minimum((i + 1) * halo_blocks, last_halo), 0)),
            pl.BlockSpec((1, tm, GROUP_WIDTH), tile),
            pl.BlockSpec((1, dm), const2),
            pl.BlockSpec(maps.shape, const3),
            pl.BlockSpec((1, POOL_WIDTH), const2),
            pl.BlockSpec(w_up_pool.shape, const2),
            pl.BlockSpec(w_up_attn.shape, const2),
            pl.BlockSpec(w_gate.shape, const2),
            pl.BlockSpec((1, 2 * dm), const2),
            pl.BlockSpec(w_out.shape, const2),
        ],
        out_specs=pl.BlockSpec((1, tm, dm), tile),
        out_shape=jax.ShapeDtypeStruct((b, s, dm), F32),
        scratch_shapes=[pltpu.VMEM((tm + 2 * POOL_HALO, POOL_WIDTH), F32)],
        compiler_params=pltpu.CompilerParams(
            dimension_semantics=("parallel", "parallel"), vmem_limit_bytes=VMEM_LIMIT_BYTES),
        name="mix",
    )(x, u, u, u, attn, norm_g, maps, pool_scale, w_up_pool, w_up_attn, w_gate, b_gate, w_out)


def _ffn_kernel(x_ref, g_ref, wg_ref, wu_ref, wd_ref, gf_ref, out_ref):
    tm = x_ref.shape[0]
    d_ff = wg_ref.shape[1]
    dot = functools.partial(jnp.dot, preferred_element_type=F32)

    def normed(r0):
        return _rms_scale(x_ref[r0:r0 + SUB_ROWS, :], g_ref[...]).astype(BF16)

    def swish_gate(parts):
        gate, up = parts
        return (gate * jax.nn.sigmoid(gate) * up).astype(BF16)

    def finish(r0, delta):
        out_ref[r0:r0 + SUB_ROWS, :] = _rms_scale(x_ref[r0:r0 + SUB_ROWS, :] + delta, gf_ref[...])

    h = normed(0)
    unfinished = None
    for r0 in range(0, tm, SUB_ROWS):
        acts, pending = [], None
        for c in range(0, d_ff, COL_CHUNK):
            cols = slice(c, c + COL_CHUNK)
            parts = (dot(h, wg_ref[:, cols]), dot(h, wu_ref[:, cols]))
            if pending is not None:
                acts.append(swish_gate(pending))
            elif unfinished is not None:
                finish(*unfinished)
            pending = parts
        acts.append(swish_gate(pending))
        if r0 + SUB_ROWS < tm:
            h = normed(r0 + SUB_ROWS)
        unfinished = (r0, dot(jnp.concatenate(acts, axis=-1), wd_ref[...]))
    finish(*unfinished)


def _ffn(x, norm_g, w_gate, w_up, w_down, norm_final):
    n, dm = x.shape
    tm = FFN_TILE
    const = lambda i: (0, 0)
    resident = functools.partial(pl.BlockSpec, index_map=const, pipeline_mode=pl.Buffered(1))
    return pl.pallas_call(
        _ffn_kernel,
        grid=(n // tm,),
        in_specs=[
            pl.BlockSpec((tm, dm), lambda i: (i, 0)),
            pl.BlockSpec((1, dm), const),
            resident(w_gate.shape),
            resident(w_up.shape),
            resident(w_down.shape),
            pl.BlockSpec((1, dm), const),
        ],
        out_specs=pl.BlockSpec((tm, dm), lambda i: (i, 0)),
        out_shape=jax.ShapeDtypeStruct((n, dm), F32),
        compiler_params=pltpu.CompilerParams(
            dimension_semantics=("parallel",), vmem_limit_bytes=VMEM_LIMIT_BYTES),
        name="ffn",
    )(x, norm_g, w_gate, w_up, w_down, norm_final)


def _encoder(x, p, rope_tabs):
    b, s, dm = x.shape
    u, qkv1, qkv2, qkv3 = _in_proj(x, p["norm_mix"], p["w_in"], rope_tabs)
    attn = _attention(qkv1, qkv2, qkv3)
    x1 = _mix(x, u, attn, p["norm_mix"], p["pool_maps"], p["pool_scale"], p["w_up_pool"], p["w_up_attn"],
              p["w_gate"], p["b_gate"], p["w_out"])
    y = _ffn(x1.reshape(b * s, dm), p["norm_ffn"], p["w_ffn_gate"], p["w_ffn_up"], p["w_ffn_down"], p["norm_final"])
    return y.reshape(b, s, dm)


def kernel(x_prompt, x_sample, norm_mix, w_in, pool_maps, pool_scale, w_up_pool, w_up_attn, w_gate, b_gate,
           w_out, norm_ffn, w_ffn_gate, w_ffn_up, w_ffn_down, norm_final):
    assert norm_mix.shape[0] == 1, "one encoder layer"
    p = {
        "norm_mix": norm_mix[0][None, :],
        "w_in": w_in[0].astype(BF16),
        "pool_maps": pool_maps[0].astype(BF16),
        "pool_scale": pool_scale[0][None, :],
        "w_up_pool": w_up_pool[0].astype(BF16),
        "w_up_attn": w_up_attn[0].astype(BF16),
        "w_gate": w_gate[0].astype(BF16),
        "b_gate": b_gate[0][None, :],
        "w_out": w_out[0].astype(BF16),
        "norm_ffn": norm_ffn[0][None, :],
        "w_ffn_gate": w_ffn_gate[0].astype(BF16),
        "w_ffn_up": w_ffn_up[0].astype(BF16),
        "w_ffn_down": w_ffn_down[0].astype(BF16),
        "norm_final": norm_final[None, :],
    }
    outs = []
    for x in (x_prompt, x_sample):
        rope_tabs = _rope_tables(x.shape[1])
        outs.append(_encoder(x, p, rope_tabs))
    return tuple(outs)
```

```python
import functools

import jax
import jax.numpy as jnp
import numpy as np
from jax import lax
from jax.experimental import pallas as pl
from jax.experimental.pallas import tpu as pltpu

HEAD_DIM = 64
HEADS_PER_GROUP = 4
ATTN_PATTERNS = ((128, 1), (512, 4), (2048, 16))
N_GROUPS = len(ATTN_PATTERNS)
GROUP_WIDTH = HEADS_PER_GROUP * HEAD_DIM
ATTN_WIDTH = N_GROUPS * GROUP_WIDTH
ROPE_DIM = HEAD_DIM // 4
ROPE_THETA = 500000.0
POOL_WINDOWS = (2, 4, 8, 16)
POOL_GROUP = 128
POOL_WIDTH = POOL_GROUP * len(POOL_WINDOWS)
POOL_HALO = 8
NORM_EPS = 1e-6

LANES = 128
Q_BLOCK = 128
BAND = 64
KEY_SPAN = Q_BLOCK + 2 * BAND
PAIRS = GROUP_WIDTH // LANES
UNITS = 8
STAGED_UNITS = 4
TOKEN_TILE = 1024
IN_TILE = 1024
FFN_TILE = 1024
SUB_ROWS = 256
COL_CHUNK = 256
WIDE_CHUNK = 512
VMEM_LIMIT_BYTES = 56 * 1024 * 1024

F32 = jnp.float32
BF16 = jnp.bfloat16


def _rms_scale(x, gain):
    return x * lax.rsqrt(jnp.mean(x * x, axis=-1, keepdims=True) + NORM_EPS) * gain


def _rope_tables(seq):
    half = ROPE_DIM // 2
    inv = (ROPE_THETA ** (-np.arange(half, dtype=np.float32) / half)).astype(np.float32).astype(np.float64)
    ang = np.arange(seq, dtype=np.float64)[:, None] * inv[None, :]
    cos, sin = np.cos(ang), np.sin(ang)
    c = np.ones((seq, HEAD_DIM))
    s_up = np.zeros((seq, HEAD_DIM))
    s_down = np.zeros((seq, HEAD_DIM))
    c[:, :half] = cos
    c[:, half:ROPE_DIM] = cos
    s_down[:, :half] = -sin
    s_up[:, half:ROPE_DIM] = sin
    rep = LANES // HEAD_DIM
    return tuple(jnp.asarray(np.tile(t, (1, rep)), dtype=F32) for t in (c, s_up, s_down))


def _in_proj_kernel(x_ref, g_ref, w_ref, c_ref, su_ref, sd_ref, u_ref, qkv1_ref, qkv2_ref, qkv3_ref, perm_ref):
    tm = x_ref.shape[1]
    n_cols = w_ref.shape[1]
    assert POOL_WIDTH % GROUP_WIDTH == 0 and WIDE_CHUNK % GROUP_WIDTH == 0
    half = ROPE_DIM // 2
    scale = HEAD_DIM ** -0.5 * np.log2(np.e)
    chunks = 3 * GROUP_WIDTH // LANES

    def normed(r0):
        return _rms_scale(x_ref[0, r0:r0 + SUB_ROWS, :], g_ref[...]).astype(BF16)

    def emit_wide(r0, col, z):
        for off in range(0, z.shape[1], GROUP_WIDTH):
            emit(r0, col + off, z[:, off:off + GROUP_WIDTH])

    def emit(r0, col, z):
        rows = slice(r0, r0 + SUB_ROWS)
        if col < POOL_WIDTH:
            u_ref[0, rows, col:col + GROUP_WIDTH] = z
            return
        part, g = divmod((col - POOL_WIDTH) // GROUP_WIDTH, N_GROUPS)
        for j in range(GROUP_WIDTH // LANES):
            t = z[:, j * LANES:(j + 1) * LANES]
            if part < 2:
                t = (t * c_ref[rows, :] + pltpu.roll(t, half, 1) * su_ref[rows, :]
                     + pltpu.roll(t, LANES - half, 1) * sd_ref[rows, :])
            if part == 0:
                t = t * scale
            dst = part * GROUP_WIDTH + j * LANES
            if g == 0:
                qkv1_ref[0, rows, dst:dst + LANES] = t.astype(BF16)
            else:
                perm_ref[(g - 1) * chunks + dst // LANES, rows, :] = t

    def deinterleave(r0):
        for g, out_ref in ((1, qkv2_ref), (2, qkv3_ref)):
            d = ATTN_PATTERNS[g][1]
            n = SUB_ROWS // d
            for r in range(d):
                for ch in range(chunks):
                    rows = perm_ref[(g - 1) * chunks + ch, pl.ds(r0 + r, n, stride=d), :]
                    out_ref[0, r, r0 // d:r0 // d + n, ch * LANES:(ch + 1) * LANES] = rows.astype(BF16)

    h = normed(0)
    for r0 in range(0, tm, SUB_ROWS):
        pending = None
        for col in range(0, n_cols, WIDE_CHUNK):
            z = jnp.dot(h, w_ref[:, col:min(col + WIDE_CHUNK, n_cols)], preferred_element_type=F32)
            if pending is not None:
                emit_wide(r0, *pending)
            elif r0 > 0:
                deinterleave(r0 - SUB_ROWS)
            pending = (col, z)
        emit_wide(r0, *pending)
        if r0 + SUB_ROWS < tm:
            h = normed(r0 + SUB_ROWS)
    deinterleave(tm - SUB_ROWS)


def _in_proj(x, norm_g, w_in, rope_tabs):
    b, s, dm = x.shape
    tm = IN_TILE
    d2, d3 = ATTN_PATTERNS[1][1], ATTN_PATTERNS[2][1]
    qkv_w = 3 * GROUP_WIDTH
    const = lambda bi, i: (0, 0)
    tab_spec = pl.BlockSpec((tm, LANES), lambda bi, i: (i, 0))
    return pl.pallas_call(
        _in_proj_kernel,
        grid=(b, s // tm),
        in_specs=[
            pl.BlockSpec((1, tm, dm), lambda bi, i: (bi, i, 0)),
            pl.BlockSpec((1, dm), const),
            pl.BlockSpec(w_in.shape, const),
            tab_spec, tab_spec, tab_spec,
        ],
        out_specs=[
            pl.BlockSpec((1, tm, POOL_WIDTH), lambda bi, i: (bi, i, 0)),
            pl.BlockSpec((1, tm, qkv_w), lambda bi, i: (bi, i, 0)),
            pl.BlockSpec((1, d2, tm // d2, qkv_w), lambda bi, i: (bi, 0, i, 0)),
            pl.BlockSpec((1, d3, tm // d3, qkv_w), lambda bi, i: (bi, 0, i, 0)),
        ],
        out_shape=[
            jax.ShapeDtypeStruct((b, s, POOL_WIDTH), F32),
            jax.ShapeDtypeStruct((b, s, qkv_w), BF16),
            jax.ShapeDtypeStruct((b, d2, s // d2, qkv_w), BF16),
            jax.ShapeDtypeStruct((b, d3, s // d3, qkv_w), BF16),
        ],
        scratch_shapes=[pltpu.VMEM((2 * qkv_w // LANES, tm, LANES), F32)],
        compiler_params=pltpu.CompilerParams(
            dimension_semantics=("parallel", "parallel"), vmem_limit_bytes=VMEM_LIMIT_BYTES),
        name="in_proj",
    )(x, norm_g, w_in, *rope_tabs)


def _attend_chunk(g, units, nk, s_ref, p_ref, o_ref, m_ref, d_ref):
    low = lax.broadcasted_iota(jnp.int32, (Q_BLOCK, LANES), 1) < HEAD_DIM
    items = [(u, j) for u in range(len(units)) for j in range(PAIRS)]

    def scores(u, j):
        load = units[u][0]
        q2 = load(0, j)
        zero = jnp.zeros_like(q2)
        stacked = jnp.concatenate([jnp.where(low, q2, zero), jnp.where(low, zero, q2)], axis=0)
        s_ref[u, j, :, :nk] = lax.dot_general(stacked, load(1, j), (((1,), (1,)), ((), ())),
                                              preferred_element_type=F32)

    def softmax(u, j):
        _, limit, rows = units[u]
        tops, dens = [], []
        for head in range(2):
            hrows = slice(head * Q_BLOCK, (head + 1) * Q_BLOCK)
            s = jnp.minimum(s_ref[u, j, hrows, :nk], limit)
            m = jnp.max(s, axis=-1, keepdims=True)
            p = jnp.exp2(s - m)
            p_ref[u, j, hrows, :nk] = p.astype(BF16)
            tops.append(m)
            dens.append(jnp.sum(p, axis=-1, keepdims=True))
        m_ref[g, j, rows, :] = jnp.where(low, tops[0], tops[1])
        d_ref[g, j, rows, :] = jnp.where(low, dens[0], dens[1])

    def values(u, j):
        load, _, rows = units[u]
        o2 = jnp.dot(p_ref[u, j, :, :nk], load(2, j), preferred_element_type=F32)
        o_ref[g, j, rows, :] = jnp.where(low, o2[:Q_BLOCK], o2[Q_BLOCK:])

    if nk < KEY_SPAN:
        for stage in (scores, softmax, values):
            for item in items:
                stage(*item)
        return

    for step in range(len(items) + 2):
        if step < len(items):
            scores(*items[step])
        if 2 <= step:
            values(*items[step - 2])
        if 1 <= step <= len(items):
            softmax(*items[step - 1])


def _attn_kernel(qkv1_ref, qkv2_ref, qkv3_ref, out_ref, o_ref, m_ref, d_ref, s_ref, p_ref, lim_ref):
    seq = out_ref.shape[1]
    gw = GROUP_WIDTH
    scratch = (s_ref, p_ref, o_ref, m_ref, d_ref)

    rel = (lax.broadcasted_iota(jnp.int32, (Q_BLOCK, KEY_SPAN), 1)
           - lax.broadcasted_iota(jnp.int32, (Q_BLOCK, KEY_SPAN), 0))
    for variant in range(3):
        lim_ref[variant] = jnp.where(jnp.abs(rel - variant * BAND) <= BAND, jnp.inf, -jnp.inf)

    def limit_of(variant, nk):
        return lim_ref[variant, :, :nk]

    def cols(part, j):
        return slice(part * gw + j * LANES, part * gw + (j + 1) * LANES)

    def chunk0(c, carry):
        units = []
        for b in range(UNITS):
            blk = c * UNITS + b
            qs = pl.multiple_of(blk * Q_BLOCK, Q_BLOCK)
            variant = jnp.where(blk == 0, 0, jnp.where(blk == seq // Q_BLOCK - 1, 2, 1))
            ws = pl.multiple_of(qs - variant * BAND, BAND)

            def load(part, j, qs=qs, ws=ws):
                rows = pl.ds(qs, Q_BLOCK) if part == 0 else pl.ds(ws, KEY_SPAN)
                return qkv1_ref[0, rows, cols(part, j)]

            units.append((load, limit_of(variant, KEY_SPAN), pl.ds(qs, Q_BLOCK)))
        _attend_chunk(0, units, KEY_SPAN, *scratch)
        return carry

    lax.fori_loop(0, seq // (UNITS * Q_BLOCK), chunk0, 0)

    d2 = ATTN_PATTERNS[1][1]
    blocks2 = seq // d2 // Q_BLOCK
    assert UNITS % blocks2 == 0 and (d2 * blocks2) % UNITS == 0

    def chunk1(c, carry):
        units = []
        for b in range(UNITS):
            r = c * (UNITS // blocks2) + b // blocks2
            qs = (b % blocks2) * Q_BLOCK
            variant = 0 if qs == 0 else 2 if b % blocks2 == blocks2 - 1 else 1
            ws = qs - variant * BAND

            def load(part, j, r=r, qs=qs, ws=ws):
                rows = pl.ds(qs, Q_BLOCK) if part == 0 else pl.ds(ws, KEY_SPAN)
                return qkv2_ref[0, r, rows, cols(part, j)]

            units.append((load, limit_of(variant, KEY_SPAN), pl.ds(qs * d2 + r, Q_BLOCK, stride=d2)))
        _attend_chunk(1, units, KEY_SPAN, *scratch)
        return carry

    lax.fori_loop(0, d2 * blocks2 // UNITS, chunk1, 0)

    d3 = ATTN_PATTERNS[2][1]
    n3 = seq // d3

    def chunk2(c, carry):
        units = []
        for b in range(STAGED_UNITS):
            r = c * STAGED_UNITS + b
            units.append((lambda part, j, r=r: qkv3_ref[0, r, :, cols(part, j)], limit_of(0, n3),
                          pl.ds(r, n3, stride=d3)))
        _attend_chunk(2, units, n3, *scratch)
        return carry

    lax.fori_loop(0, d3 // STAGED_UNITS, chunk2, 0)

    def merge(i, carry):
        rows = pl.ds(pl.multiple_of(i * Q_BLOCK, Q_BLOCK), Q_BLOCK)
        for j in range(PAIRS):
            tops = [m_ref[g, j, rows, :] for g in range(N_GROUPS)]
            top = functools.reduce(jnp.maximum, tops)
            scales = [jnp.exp2(t - top) for t in tops]
            num = sum(e * o_ref[g, j, rows, :] for g, e in enumerate(scales))
            den = sum(e * d_ref[g, j, rows, :] for g, e in enumerate(scales))
            out_ref[0, rows, j * LANES:(j + 1) * LANES] = (num / den).astype(out_ref.dtype)
        return carry

    lax.fori_loop(0, seq // Q_BLOCK, merge, 0)


def _attention(qkv1, qkv2, qkv3):
    b, s, _ = qkv1.shape
    assert s // ATTN_PATTERNS[2][1] == Q_BLOCK and s % (UNITS * Q_BLOCK) == 0 and ATTN_PATTERNS[2][1] % STAGED_UNITS == 0
    assert all(w // 2 // d == BAND for w, d in ATTN_PATTERNS)
    return pl.pallas_call(
        _attn_kernel,
        grid=(b,),
        in_specs=[
            pl.BlockSpec((1,) + qkv1.shape[1:], lambda bi: (bi, 0, 0)),
            pl.BlockSpec((1,) + qkv2.shape[1:], lambda bi: (bi, 0, 0, 0)),
            pl.BlockSpec((1,) + qkv3.shape[1:], lambda bi: (bi, 0, 0, 0)),
        ],
        out_specs=pl.BlockSpec((1, s, GROUP_WIDTH), lambda bi: (bi, 0, 0)),
        out_shape=jax.ShapeDtypeStruct((b, s, GROUP_WIDTH), BF16),
        scratch_shapes=[
            pltpu.VMEM((N_GROUPS, PAIRS, s, LANES), F32),
            pltpu.VMEM((N_GROUPS, PAIRS, s, LANES), F32),
            pltpu.VMEM((N_GROUPS, PAIRS, s, LANES), F32),
            pltpu.VMEM((UNITS, PAIRS, 2 * Q_BLOCK, KEY_SPAN), F32),
            pltpu.VMEM((UNITS, PAIRS, 2 * Q_BLOCK, KEY_SPAN), BF16),
            pltpu.VMEM((3, Q_BLOCK, KEY_SPAN), F32),
        ],
        compiler_params=pltpu.CompilerParams(
            dimension_semantics=("parallel",), vmem_limit_bytes=VMEM_LIMIT_BYTES),
        name="attention",
    )(qkv1, qkv2, qkv3)


def _mix_kernel(x_ref, u_ref, up_ref, un_ref, at_ref, g_ref, maps_ref, ps_ref, wup_ref, wua_ref, wg_ref, bg_ref,
                wo_ref, out_ref, ext_ref):
    tm = x_ref.shape[1]
    dm = x_ref.shape[2]
    i = pl.program_id(1)
    seq = tm * pl.num_programs(1)

    ext_ref[0:POOL_HALO, :] = jnp.where(i > 0, up_ref[0], 0.0)
    ext_ref[POOL_HALO:POOL_HALO + tm, :] = u_ref[0]
    ext_ref[POOL_HALO + tm:, :] = jnp.where(i < pl.num_programs(1) - 1, un_ref[0], 0.0)

    n_chunks = dm // COL_CHUNK
    assert n_chunks == len(POOL_WINDOWS), "one pooling group per column chunk"

    def normed(r0):
        return _rms_scale(x_ref[0, r0:r0 + SUB_ROWS, :], g_ref[...]).astype(BF16)

    def pool_group(r0, j):
        w = POOL_WINDOWS[j]
        cols = slice(j * POOL_GROUP, (j + 1) * POOL_GROUP)
        pos = i * tm + r0 + lax.broadcasted_iota(jnp.int32, (SUB_ROWS, 1), 0)
        span = SUB_ROWS + 2 * POOL_HALO
        ext = ext_ref[r0:r0 + span, cols]
        shifted = lambda a, k: pltpu.roll(a, span - k, 0)
        run, width = ext, 1
        while 2 * width < w:
            run, width = run + shifted(run, width), 2 * width
        start = POOL_HALO - w // 2
        assert 2 * width == w and start + width == POOL_HALO
        lead = run if start == 0 else shifted(run, start)
        total = lead[:SUB_ROWS] + run[POOL_HALO:POOL_HALO + SUB_ROWS]
        cnt = (jnp.minimum(pos + w // 2, seq) - jnp.maximum(pos - w // 2, 0)).astype(F32)
        pooled = total / cnt - ext[POOL_HALO:POOL_HALO + SUB_ROWS]
        return pooled.astype(BF16)

    def map_group(pooled, j):
        mg = jnp.dot(pooled, maps_ref[j], preferred_element_type=F32)
        return (mg * ps_ref[:, j * POOL_GROUP:(j + 1) * POOL_GROUP]).astype(BF16)

    def project(h, mixed, attn, c):
        ca, cb = slice(c, c + COL_CHUNK), slice(dm + c, dm + c + COL_CHUNK)
        dot = functools.partial(jnp.dot, preferred_element_type=F32)
        return (dot(h, wg_ref[:, ca]) + bg_ref[:, ca], dot(h, wg_ref[:, cb]) + bg_ref[:, cb],
                dot(mixed, wup_ref[:, ca]), dot(attn, wua_ref[:, ca]))

    def gate(parts):
        ga, gb, a, b = parts
        return (jax.nn.sigmoid(ga) * a + jax.nn.sigmoid(gb) * b).astype(BF16)

    h = normed(0)
    mixed = jnp.concatenate([map_group(pool_group(0, j), j) for j in range(n_chunks)], axis=-1)
    for r0 in range(0, tm, SUB_ROWS):
        more = r0 + SUB_ROWS < tm
        attn = at_ref[0, r0:r0 + SUB_ROWS, :]
        ys, pending, pooled, mapped = [], None, [], []
        for j in range(n_chunks):
            parts = project(h, mixed, attn, j * COL_CHUNK)
            if pooled:
                mapped.append(map_group(pooled[-1], j - 1))
            if pending is not None:
                ys.append(gate(pending))
            pending = parts
            if more:
                pooled.append(pool_group(r0 + SUB_ROWS, j))
        ys.append(gate(pending))
        delta = jnp.dot(jnp.concatenate(ys, axis=-1), wo_ref[...], preferred_element_type=F32)
        if more:
            mapped.append(map_group(pooled[-1], n_chunks - 1))
            h = normed(r0 + SUB_ROWS)
            mixed = jnp.concatenate(mapped, axis=-1)
        out_ref[0, r0:r0 + SUB_ROWS, :] = x_ref[0, r0:r0 + SUB_ROWS, :] + delta


def _mix(x, u, attn, norm_g, maps, pool_scale, w_up_pool, w_up_attn, w_gate, b_gate, w_out):
    b, s, dm = x.shape
    tm = TOKEN_TILE
    halo_blocks = tm // POOL_HALO
    last_halo = s // POOL_HALO - 1
    tile = lambda bi, i: (bi, i, 0)
    const2 = lambda bi, i: (0, 0)
    const3 = lambda bi, i: (0, 0, 0)
    return pl.pallas_call(
        _mix_kernel,
        grid=(b, s // tm),
        in_specs=[
            pl.BlockSpec((1, tm, dm), tile),
            pl.BlockSpec((1, tm, POOL_WIDTH), tile),
            pl.BlockSpec((1, POOL_HALO, POOL_WIDTH), lambda bi, i: (bi, jnp.maximum(i * halo_blocks - 1, 0), 0)),
            pl.BlockSpec((1, POOL_HALO, POOL_WIDTH),
                         lambda bi, i: (bi, jnp.minimum((i + 1) * halo_blocks, last_halo), 0)),
            pl.BlockSpec((1, tm, GROUP_WIDTH), tile),
            pl.BlockSpec((1, dm), const2),
            pl.BlockSpec(maps.shape, const3),
            pl.BlockSpec((1, POOL_WIDTH), const2),
            pl.BlockSpec(w_up_pool.shape, const2),
            pl.BlockSpec(w_up_attn.shape, const2),
            pl.BlockSpec(w_gate.shape, const2),
            pl.BlockSpec((1, 2 * dm), const2),
            pl.BlockSpec(w_out.shape, const2),
        ],
        out_specs=pl.BlockSpec((1, tm, dm), tile),
        out_shape=jax.ShapeDtypeStruct((b, s, dm), F32),
        scratch_shapes=[pltpu.VMEM((tm + 2 * POOL_HALO, POOL_WIDTH), F32)],
        compiler_params=pltpu.CompilerParams(
            dimension_semantics=("parallel", "parallel"), vmem_limit_bytes=VMEM_LIMIT_BYTES),
        name="mix",
    )(x, u, u, u, attn, norm_g, maps, pool_scale, w_up_pool, w_up_attn, w_gate, b_gate, w_out)


def _ffn_kernel(x_ref, g_ref, wg_ref, wu_ref, wd_ref, gf_ref, out_ref):
    tm = x_ref.shape[0]
    d_ff = wg_ref.shape[1]
    dot = functools.partial(jnp.dot, preferred_element_type=F32)

    def normed(r0):
        return _rms_scale(x_ref[r0:r0 + SUB_ROWS, :], g_ref[...]).astype(BF16)

    def swish_gate(parts):
        gate, up = parts
        return (gate * jax.nn.sigmoid(gate) * up).astype(BF16)

    def finish(r0, delta):
        out_ref[r0:r0 + SUB_ROWS, :] = _rms_scale(x_ref[r0:r0 + SUB_ROWS, :] + delta, gf_ref[...])

    h = normed(0)
    unfinished = None
    for r0 in range(0, tm, SUB_ROWS):
        acts, pending = [], None
        for c in range(0, d_ff, COL_CHUNK):
            cols = slice(c, c + COL_CHUNK)
            parts = (dot(h, wg_ref[:, cols]), dot(h, wu_ref[:, cols]))
            if pending is not None:
                acts.append(swish_gate(pending))
            elif unfinished is not None:
                finish(*unfinished)
            pending = parts
        acts.append(swish_gate(pending))
        if r0 + SUB_ROWS < tm:
            h = normed(r0 + SUB_ROWS)
        unfinished = (r0, dot(jnp.concatenate(acts, axis=-1), wd_ref[...]))
    finish(*unfinished)


def _ffn(x, norm_g, w_gate, w_up, w_down, norm_final):
    n, dm = x.shape
    tm = FFN_TILE
    const = lambda i: (0, 0)
    resident = functools.partial(pl.BlockSpec, index_map=const, pipeline_mode=pl.Buffered(1))
    return pl.pallas_call(
        _ffn_kernel,
        grid=(n // tm,),
        in_specs=[
            pl.BlockSpec((tm, dm), lambda i: (i, 0)),
            pl.BlockSpec((1, dm), const),
            resident(w_gate.shape),
            resident(w_up.shape),
            resident(w_down.shape),
            pl.BlockSpec((1, dm), const),
        ],
        out_specs=pl.BlockSpec((tm, dm), lambda i: (i, 0)),
        out_shape=jax.ShapeDtypeStruct((n, dm), F32),
        compiler_params=pltpu.CompilerParams(
            dimension_semantics=("parallel",), vmem_limit_bytes=VMEM_LIMIT_BYTES),
        name="ffn",
    )(x, norm_g, w_gate, w_up, w_down, norm_final)


def _encoder(x, p, rope_tabs):
    b, s, dm = x.shape
    u, qkv1, qkv2, qkv3 = _in_proj(x, p["norm_mix"], p["w_in"], rope_tabs)
    attn = _attention(qkv1, qkv2, qkv3)
    x1 = _mix(x, u, attn, p["norm_mix"], p["pool_maps"], p["pool_scale"], p["w_up_pool"], p["w_up_attn"],
              p["w_gate"], p["b_gate"], p["w_out"])
    y = _ffn(x1.reshape(b * s, dm), p["norm_ffn"], p["w_ffn_gate"], p["w_ffn_up"], p["w_ffn_down"], p["norm_final"])
    return y.reshape(b, s, dm)


def kernel(x_prompt, x_sample, norm_mix, w_in, pool_maps, pool_scale, w_up_pool, w_up_attn, w_gate, b_gate,
           w_out, norm_ffn, w_ffn_gate, w_ffn_up, w_ffn_down, norm_final):
    assert norm_mix.shape[0] == 1, "one encoder layer"
    p = {
        "norm_mix": norm_mix[0][None, :],
        "w_in": w_in[0].astype(BF16),
        "pool_maps": pool_maps[0].astype(BF16),
        "pool_scale": pool_scale[0][None, :],
        "w_up_pool": w_up_pool[0].astype(BF16),
        "w_up_attn": w_up_attn[0].astype(BF16),
        "w_gate": w_gate[0].astype(BF16),
        "b_gate": b_gate[0][None, :],
        "w_out": w_out[0].astype(BF16),
        "norm_ffn": norm_ffn[0][None, :],
        "w_ffn_gate": w_ffn_gate[0].astype(BF16),
        "w_ffn_up": w_ffn_up[0].astype(BF16),
        "w_ffn_down": w_ffn_down[0].astype(BF16),
        "norm_final": norm_final[None, :],
    }
    outs = []
    for x in (x_prompt, x_sample):
        rope_tabs = _rope_tables(x.shape[1])
        outs.append(_encoder(x, p, rope_tabs))
    return tuple(outs)
```

```python
import functools

import jax
import jax.numpy as jnp
import numpy as np
from jax import lax
from jax.experimental import pallas as pl
from jax.experimental.pallas import tpu as pltpu

HEAD_DIM = 64
HEADS_PER_GROUP = 4
ATTN_PATTERNS = ((128, 1), (512, 4), (2048, 16))
N_GROUPS = len(ATTN_PATTERNS)
GROUP_WIDTH = HEADS_PER_GROUP * HEAD_DIM
ATTN_WIDTH = N_GROUPS * GROUP_WIDTH
ROPE_DIM = HEAD_DIM // 4
ROPE_THETA = 500000.0
POOL_WINDOWS = (2, 4, 8, 16)
POOL_GROUP = 128
POOL_WIDTH = POOL_GROUP * len(POOL_WINDOWS)
POOL_HALO = 8
NORM_EPS = 1e-6

LANES = 128
Q_BLOCK = 128
BAND = 64
KEY_SPAN = Q_BLOCK + 2 * BAND
PAIRS = GROUP_WIDTH // LANES
UNITS = 16
STAGED_UNITS = 4
TOKEN_TILE = 1024
IN_TILE = 1024
FFN_TILE = 1024
SUB_ROWS = 256
COL_CHUNK = 256
WIDE_CHUNK = 512
VMEM_LIMIT_BYTES = 56 * 1024 * 1024

F32 = jnp.float32
BF16 = jnp.bfloat16


def _rms_scale(x, gain):
    return x * lax.rsqrt(jnp.mean(x * x, axis=-1, keepdims=True) + NORM_EPS) * gain


def _rope_tables(seq):
    half = ROPE_DIM // 2
    inv = (ROPE_THETA ** (-np.arange(half, dtype=np.float32) / half)).astype(np.float32).astype(np.float64)
    ang = np.arange(seq, dtype=np.float64)[:, None] * inv[None, :]
    cos, sin = np.cos(ang), np.sin(ang)
    c = np.ones((seq, HEAD_DIM))
    s_up = np.zeros((seq, HEAD_DIM))
    s_down = np.zeros((seq, HEAD_DIM))
    c[:, :half] = cos
    c[:, half:ROPE_DIM] = cos
    s_down[:, :half] = -sin
    s_up[:, half:ROPE_DIM] = sin
    rep = LANES // HEAD_DIM
    return tuple(jnp.asarray(np.tile(t, (1, rep)), dtype=F32) for t in (c, s_up, s_down))


def _in_proj_kernel(x_ref, g_ref, w_ref, c_ref, su_ref, sd_ref, u_ref, qkv1_ref, qkv2_ref, qkv3_ref, perm_ref):
    tm = x_ref.shape[1]
    n_cols = w_ref.shape[1]
    assert POOL_WIDTH % GROUP_WIDTH == 0 and WIDE_CHUNK % GROUP_WIDTH == 0
    half = ROPE_DIM // 2
    scale = HEAD_DIM ** -0.5 * np.log2(np.e)
    chunks = 3 * GROUP_WIDTH // LANES

    def normed(r0):
        return _rms_scale(x_ref[0, r0:r0 + SUB_ROWS, :], g_ref[...]).astype(BF16)

    def emit_wide(r0, col, z):
        for off in range(0, z.shape[1], GROUP_WIDTH):
            emit(r0, col + off, z[:, off:off + GROUP_WIDTH])

    def emit(r0, col, z):
        rows = slice(r0, r0 + SUB_ROWS)
        if col < POOL_WIDTH:
            u_ref[0, rows, col:col + GROUP_WIDTH] = z
            return
        part, g = divmod((col - POOL_WIDTH) // GROUP_WIDTH, N_GROUPS)
        for j in range(GROUP_WIDTH // LANES):
            t = z[:, j * LANES:(j + 1) * LANES]
            if part < 2:
                t = (t * c_ref[rows, :] + pltpu.roll(t, half, 1) * su_ref[rows, :]
                     + pltpu.roll(t, LANES - half, 1) * sd_ref[rows, :])
            if part == 0:
                t = t * scale
            dst = part * GROUP_WIDTH + j * LANES
            if g == 0:
                qkv1_ref[0, rows, dst:dst + LANES] = t.astype(BF16)
            else:
                perm_ref[(g - 1) * chunks + dst // LANES, rows, :] = t

    def deinterleave(r0):
        for g, out_ref in ((1, qkv2_ref), (2, qkv3_ref)):
            d = ATTN_PATTERNS[g][1]
            n = SUB_ROWS // d
            for r in range(d):
                for ch in range(chunks):
                    rows = perm_ref[(g - 1) * chunks + ch, pl.ds(r0 + r, n, stride=d), :]
                    out_ref[0, r, r0 // d:r0 // d + n, ch * LANES:(ch + 1) * LANES] = rows.astype(BF16)

    h = normed(0)
    for r0 in range(0, tm, SUB_ROWS):
        pending = None
        for col in range(0, n_cols, WIDE_CHUNK):
            z = jnp.dot(h, w_ref[:, col:min(col + WIDE_CHUNK, n_cols)], preferred_element_type=F32)
            if pending is not None:
                emit_wide(r0, *pending)
            elif r0 > 0:
                deinterleave(r0 - SUB_ROWS)
            pending = (col, z)
        emit_wide(r0, *pending)
        if r0 + SUB_ROWS < tm:
            h = normed(r0 + SUB_ROWS)
    deinterleave(tm - SUB_ROWS)


def _in_proj(x, norm_g, w_in, rope_tabs):
    b, s, dm = x.shape
    tm = IN_TILE
    d2, d3 = ATTN_PATTERNS[1][1], ATTN_PATTERNS[2][1]
    qkv_w = 3 * GROUP_WIDTH
    const = lambda bi, i: (0, 0)
    tab_spec = pl.BlockSpec((tm, LANES), lambda bi, i: (i, 0))
    return pl.pallas_call(
        _in_proj_kernel,
        grid=(b, s // tm),
        in_specs=[
            pl.BlockSpec((1, tm, dm), lambda bi, i: (bi, i, 0)),
            pl.BlockSpec((1, dm), const),
            pl.BlockSpec(w_in.shape, const),
            tab_spec, tab_spec, tab_spec,
        ],
        out_specs=[
            pl.BlockSpec((1, tm, POOL_WIDTH), lambda bi, i: (bi, i, 0)),
            pl.BlockSpec((1, tm, qkv_w), lambda bi, i: (bi, i, 0)),
            pl.BlockSpec((1, d2, tm // d2, qkv_w), lambda bi, i: (bi, 0, i, 0)),
            pl.BlockSpec((1, d3, tm // d3, qkv_w), lambda bi, i: (bi, 0, i, 0)),
        ],
        out_shape=[
            jax.ShapeDtypeStruct((b, s, POOL_WIDTH), F32),
            jax.ShapeDtypeStruct((b, s, qkv_w), BF16),
            jax.ShapeDtypeStruct((b, d2, s // d2, qkv_w), BF16),
            jax.ShapeDtypeStruct((b, d3, s // d3, qkv_w), BF16),
        ],
        scratch_shapes=[pltpu.VMEM((2 * qkv_w // LANES, tm, LANES), F32)],
        compiler_params=pltpu.CompilerParams(
            dimension_semantics=("parallel", "parallel"), vmem_limit_bytes=VMEM_LIMIT_BYTES),
        name="in_proj",
    )(x, norm_g, w_in, *rope_tabs)


def _attend_chunk(g, units, nk, s_ref, p_ref, o_ref, m_ref, d_ref):
    low = lax.broadcasted_iota(jnp.int32, (Q_BLOCK, LANES), 1) < HEAD_DIM
    items = [(u, j) for u in range(len(units)) for j in range(PAIRS)]

    def scores(u, j):
        load = units[u][0]
        q2 = load(0, j)
        zero = jnp.zeros_like(q2)
        stacked = jnp.concatenate([jnp.where(low, q2, zero), jnp.where(low, zero, q2)], axis=0)
        s_ref[u, j, :, :nk] = lax.dot_general(stacked, load(1, j), (((1,), (1,)), ((), ())),
                                              preferred_element_type=F32)

    def softmax(u, j):
        _, limit, rows = units[u]
        tops, dens = [], []
        for head in range(2):
            hrows = slice(head * Q_BLOCK, (head + 1) * Q_BLOCK)
            s = jnp.minimum(s_ref[u, j, hrows, :nk], limit)
            m = jnp.max(s, axis=-1, keepdims=True)
            p = jnp.exp2(s - m)
            p_ref[u, j, hrows, :nk] = p.astype(BF16)
            tops.append(m)
            dens.append(jnp.sum(p, axis=-1, keepdims=True))
        m_ref[g, j, rows, :] = jnp.where(low, tops[0], tops[1])
        d_ref[g, j, rows, :] = jnp.where(low, dens[0], dens[1])

    def values(u, j):
        load, _, rows = units[u]
        o2 = jnp.dot(p_ref[u, j, :, :nk], load(2, j), preferred_element_type=F32)
        o_ref[g, j, rows, :] = jnp.where(low, o2[:Q_BLOCK], o2[Q_BLOCK:])

    if nk < KEY_SPAN:
        for stage in (scores, softmax, values):
            for item in items:
                stage(*item)
        return

    for step in range(len(items) + 2):
        if step < len(items):
            scores(*items[step])
        if 2 <= step:
            values(*items[step - 2])
        if 1 <= step <= len(items):
            softmax(*items[step - 1])


def _attn_kernel(qkv1_ref, qkv2_ref, qkv3_ref, out_ref, o_ref, m_ref, d_ref, s_ref, p_ref, lim_ref):
    seq = out_ref.shape[1]
    gw = GROUP_WIDTH
    scratch = (s_ref, p_ref, o_ref, m_ref, d_ref)

    rel = (lax.broadcasted_iota(jnp.int32, (Q_BLOCK, KEY_SPAN), 1)
           - lax.broadcasted_iota(jnp.int32, (Q_BLOCK, KEY_SPAN), 0))
    for variant in range(3):
        lim_ref[variant] = jnp.where(jnp.abs(rel - variant * BAND) <= BAND, jnp.inf, -jnp.inf)

    def limit_of(variant, nk):
        return lim_ref[variant, :, :nk]

    def cols(part, j):
        return slice(part * gw + j * LANES, part * gw + (j + 1) * LANES)

    def chunk0(c, carry):
        units = []
        for b in range(UNITS):
            blk = c * UNITS + b
            qs = pl.multiple_of(blk * Q_BLOCK, Q_BLOCK)
            variant = jnp.where(blk == 0, 0, jnp.where(blk == seq // Q_BLOCK - 1, 2, 1))
            ws = pl.multiple_of(qs - variant * BAND, BAND)

            def load(part, j, qs=qs, ws=ws):
                rows = pl.ds(qs, Q_BLOCK) if part == 0 else pl.ds(ws, KEY_SPAN)
                return qkv1_ref[0, rows, cols(part, j)]

            units.append((load, limit_of(variant, KEY_SPAN), pl.ds(qs, Q_BLOCK)))
        _attend_chunk(0, units, KEY_SPAN, *scratch)
        return carry

    lax.fori_loop(0, seq // (UNITS * Q_BLOCK), chunk0, 0)

    d2 = ATTN_PATTERNS[1][1]
    blocks2 = seq // d2 // Q_BLOCK
    assert UNITS % blocks2 == 0 and (d2 * blocks2) % UNITS == 0

    def chunk1(c, carry):
        units = []
        for b in range(UNITS):
            r = c * (UNITS // blocks2) + b // blocks2
            qs = (b % blocks2) * Q_BLOCK
            variant = 0 if qs == 0 else 2 if b % blocks2 == blocks2 - 1 else 1
            ws = qs - variant * BAND

            def load(part, j, r=r, qs=qs, ws=ws):
                rows = pl.ds(qs, Q_BLOCK) if part == 0 else pl.ds(ws, KEY_SPAN)
                return qkv2_ref[0, r, rows, cols(part, j)]

            units.append((load, limit_of(variant, KEY_SPAN), pl.ds(qs * d2 + r, Q_BLOCK, stride=d2)))
        _attend_chunk(1, units, KEY_SPAN, *scratch)
        return carry

    lax.fori_loop(0, d2 * blocks2 // UNITS, chunk1, 0)

    d3 = ATTN_PATTERNS[2][1]
    n3 = seq // d3

    def chunk2(c, carry):
        units = []
        for b in range(STAGED_UNITS):
            r = c * STAGED_UNITS + b
            units.append((lambda part, j, r=r: qkv3_ref[0, r, :, cols(part, j)], limit_of(0, n3),
                          pl.ds(r, n3, stride=d3)))
        _attend_chunk(2, units, n3, *scratch)
        return carry

    lax.fori_loop(0, d3 // STAGED_UNITS, chunk2, 0)

    def merge(i, carry):
        rows = pl.ds(pl.multiple_of(i * Q_BLOCK, Q_BLOCK), Q_BLOCK)
        for j in range(PAIRS):
            tops = [m_ref[g, j, rows, :] for g in range(N_GROUPS)]
            top = functools.reduce(jnp.maximum, tops)
            scales = [jnp.exp2(t - top) for t in tops]
            num = sum(e * o_ref[g, j, rows, :] for g, e in enumerate(scales))
            den = sum(e * d_ref[g, j, rows, :] for g, e in enumerate(scales))
            out_ref[0, rows, j * LANES:(j + 1) * LANES] = (num / den).astype(out_ref.dtype)
        return carry

    lax.fori_loop(0, seq // Q_BLOCK, merge, 0)


def _attention(qkv1, qkv2, qkv3):
    b, s, _ = qkv1.shape
    assert s // ATTN_PATTERNS[2][1] == Q_BLOCK and s % (UNITS * Q_BLOCK) == 0 and ATTN_PATTERNS[2][1] % STAGED_UNITS == 0
    assert all(w // 2 // d == BAND for w, d in ATTN_PATTERNS)
    return pl.pallas_call(
        _attn_kernel,
        grid=(b,),
        in_specs=[
            pl.BlockSpec((1,) + qkv1.shape[1:], lambda bi: (bi, 0, 0)),
            pl.BlockSpec((1,) + qkv2.shape[1:], lambda bi: (bi, 0, 0, 0)),
            pl.BlockSpec((1,) + qkv3.shape[1:], lambda bi: (bi, 0, 0, 0)),
        ],
        out_specs=pl.BlockSpec((1, s, GROUP_WIDTH), lambda bi: (bi, 0, 0)),
        out_shape=jax.ShapeDtypeStruct((b, s, GROUP_WIDTH), BF16),
        scratch_shapes=[
            pltpu.VMEM((N_GROUPS, PAIRS, s, LANES), F32),
            pltpu.VMEM((N_GROUPS, PAIRS, s, LANES), F32),
            pltpu.VMEM((N_GROUPS, PAIRS, s, LANES), F32),
            pltpu.VMEM((UNITS, PAIRS, 2 * Q_BLOCK, KEY_SPAN), F32),
            pltpu.VMEM((UNITS, PAIRS, 2 * Q_BLOCK, KEY_SPAN), BF16),
            pltpu.VMEM((3, Q_BLOCK, KEY_SPAN), F32),
        ],
        compiler_params=pltpu.CompilerParams(
            dimension_semantics=("parallel",), vmem_limit_bytes=VMEM_LIMIT_BYTES),
        name="attention",
    )(qkv1, qkv2, qkv3)


def _mix_kernel(x_ref, u_ref, up_ref, un_ref, at_ref, g_ref, maps_ref, ps_ref, wup_ref, wua_ref, wg_ref, bg_ref,
                wo_ref, out_ref, ext_ref):
    tm = x_ref.shape[1]
    dm = x_ref.shape[2]
    i = pl.program_id(1)
    seq = tm * pl.num_programs(1)

    ext_ref[0:POOL_HALO, :] = jnp.where(i > 0, up_ref[0], 0.0)
    ext_ref[POOL_HALO:POOL_HALO + tm, :] = u_ref[0]
    ext_ref[POOL_HALO + tm:, :] = jnp.where(i < pl.num_programs(1) - 1, un_ref[0], 0.0)

    n_chunks = dm // COL_CHUNK
    assert n_chunks == len(POOL_WINDOWS), "one pooling group per column chunk"

    def normed(r0):
        return _rms_scale(x_ref[0, r0:r0 + SUB_ROWS, :], g_ref[...]).astype(BF16)

    def pool_group(r0, j):
        w = POOL_WINDOWS[j]
        cols = slice(j * POOL_GROUP, (j + 1) * POOL_GROUP)
        pos = i * tm + r0 + lax.broadcasted_iota(jnp.int32, (SUB_ROWS, 1), 0)
        span = SUB_ROWS + 2 * POOL_HALO
        ext = ext_ref[r0:r0 + span, cols]
        shifted = lambda a, k: pltpu.roll(a, span - k, 0)
        run, width = ext, 1
        while 2 * width < w:
            run, width = run + shifted(run, width), 2 * width
        start = POOL_HALO - w // 2
        assert 2 * width == w and start + width == POOL_HALO
        lead = run if start == 0 else shifted(run, start)
        total = lead[:SUB_ROWS] + run[POOL_HALO:POOL_HALO + SUB_ROWS]
        cnt = (jnp.minimum(pos + w // 2, seq) - jnp.maximum(pos - w // 2, 0)).astype(F32)
        pooled = total / cnt - ext[POOL_HALO:POOL_HALO + SUB_ROWS]
        return pooled.astype(BF16)

    def map_group(pooled, j):
        mg = jnp.dot(pooled, maps_ref[j], preferred_element_type=F32)
        return (mg * ps_ref[:, j * POOL_GROUP:(j + 1) * POOL_GROUP]).astype(BF16)

    def project(h, mixed, attn, c):
        ca, cb = slice(c, c + COL_CHUNK), slice(dm + c, dm + c + COL_CHUNK)
        dot = functools.partial(jnp.dot, preferred_element_type=F32)
        return (dot(h, wg_ref[:, ca]) + bg_ref[:, ca], dot(h, wg_ref[:, cb]) + bg_ref[:, cb],
                dot(mixed, wup_ref[:, ca]), dot(attn, wua_ref[:, ca]))

    def gate(parts):
        ga, gb, a, b = parts
        return (jax.nn.sigmoid(ga) * a + jax.nn.sigmoid(gb) * b).astype(BF16)

    h = normed(0)
    mixed = jnp.concatenate([map_group(pool_group(0, j), j) for j in range(n_chunks)], axis=-1)
    for r0 in range(0, tm, SUB_ROWS):
        more = r0 + SUB_ROWS < tm
        attn = at_ref[0, r0:r0 + SUB_ROWS, :]
        ys, pending, pooled, mapped = [], None, [], []
        for j in range(n_chunks):
            parts = project(h, mixed, attn, j * COL_CHUNK)
            if pooled:
                mapped.append(map_group(pooled[-1], j - 1))
            if pending is not None:
                ys.append(gate(pending))
            pending = parts
            if more:
                pooled.append(pool_group(r0 + SUB_ROWS, j))
        ys.append(gate(pending))
        delta = jnp.dot(jnp.concatenate(ys, axis=-1), wo_ref[...], preferred_element_type=F32)
        if more:
            mapped.append(map_group(pooled[-1], n_chunks - 1))
            h = normed(r0 + SUB_ROWS)
            mixed = jnp.concatenate(mapped, axis=-1)
        out_ref[0, r0:r0 + SUB_ROWS, :] = x_ref[0, r0:r0 + SUB_ROWS, :] + delta


def _mix(x, u, attn, norm_g, maps, pool_scale, w_up_pool, w_up_attn, w_gate, b_gate, w_out):
    b, s, dm = x.shape
    tm = TOKEN_TILE
    halo_blocks = tm // POOL_HALO
    last_halo = s // POOL_HALO - 1
    tile = lambda bi, i: (bi, i, 0)
    const2 = lambda bi, i: (0, 0)
    const3 = lambda bi, i: (0, 0, 0)
    return pl.pallas_call(
        _mix_kernel,
        grid=(b, s // tm),
        in_specs=[
            pl.BlockSpec((1, tm, dm), tile),
            pl.BlockSpec((1, tm, POOL_WIDTH), tile),
            pl.BlockSpec((1, POOL_HALO, POOL_WIDTH), lambda bi, i: (bi, jnp.maximum(i * halo_blocks - 1, 0), 0)),
            pl.BlockSpec((1, POOL_HALO, POOL_WIDTH),
                         lambda bi, i: (bi, jnp.minimum((i + 1) * halo_blocks, last_halo), 0)),
            pl.BlockSpec((1, tm, GROUP_WIDTH), tile),
            pl.BlockSpec((1, dm), const2),
            pl.BlockSpec(maps.shape, const3),
            pl.BlockSpec((1, POOL_WIDTH), const2),
            pl.BlockSpec(w_up_pool.shape, const2),
            pl.BlockSpec(w_up_attn.shape, const2),
            pl.BlockSpec(w_gate.shape, const2),
            pl.BlockSpec((1, 2 * dm), const2),
            pl.BlockSpec(w_out.shape, const2),
        ],
        out_specs=pl.BlockSpec((1, tm, dm), tile),
        out_shape=jax.ShapeDtypeStruct((b, s, dm), F32),
        scratch_shapes=[pltpu.VMEM((tm + 2 * POOL_HALO, POOL_WIDTH), F32)],
        compiler_params=pltpu.CompilerParams(
            dimension_semantics=("parallel", "parallel"), vmem_limit_bytes=VMEM_LIMIT_BYTES),
        name="mix",
    )(x, u, u, u, attn, norm_g, maps, pool_scale, w_up_pool, w_up_attn, w_gate, b_gate, w_out)


def _ffn_kernel(x_ref, g_ref, wg_ref, wu_ref, wd_ref, gf_ref, out_ref):
    tm = x_ref.shape[0]
    d_ff = wg_ref.shape[1]
    dot = functools.partial(jnp.dot, preferred_element_type=F32)

    def normed(r0):
        return _rms_scale(x_ref[r0:r0 + SUB_ROWS, :], g_ref[...]).astype(BF16)

    def swish_gate(parts):
        gate, up = parts
        return (gate * jax.nn.sigmoid(gate) * up).astype(BF16)

    def finish(r0, delta):
        out_ref[r0:r0 + SUB_ROWS, :] = _rms_scale(x_ref[r0:r0 + SUB_ROWS, :] + delta, gf_ref[...])

    h = normed(0)
    unfinished = None
    for r0 in range(0, tm, SUB_ROWS):
        acts, pending = [], None
        for c in range(0, d_ff, COL_CHUNK):
            cols = slice(c, c + COL_CHUNK)
            parts = (dot(h, wg_ref[:, cols]), dot(h, wu_ref[:, cols]))
            if pending is not None:
                acts.append(swish_gate(pending))
            elif unfinished is not None:
                finish(*unfinished)
            pending = parts
        acts.append(swish_gate(pending))
        if r0 + SUB_ROWS < tm:
            h = normed(r0 + SUB_ROWS)
        unfinished = (r0, dot(jnp.concatenate(acts, axis=-1), wd_ref[...]))
    finish(*unfinished)


def _ffn(x, norm_g, w_gate, w_up, w_down, norm_final):
    n, dm = x.shape
    tm = FFN_TILE
    const = lambda i: (0, 0)
    resident = functools.partial(pl.BlockSpec, index_map=const, pipeline_mode=pl.Buffered(1))
    return pl.pallas_call(
        _ffn_kernel,
        grid=(n // tm,),
        in_specs=[
            pl.BlockSpec((tm, dm), lambda i: (i, 0)),
            pl.BlockSpec((1, dm), const),
            resident(w_gate.shape),
            resident(w_up.shape),
            resident(w_down.shape),
            pl.BlockSpec((1, dm), const),
        ],
        out_specs=pl.BlockSpec((tm, dm), lambda i: (i, 0)),
        out_shape=jax.ShapeDtypeStruct((n, dm), F32),
        compiler_params=pltpu.CompilerParams(
            dimension_semantics=("parallel",), vmem_limit_bytes=VMEM_LIMIT_BYTES),
        name="ffn",
    )(x, norm_g, w_gate, w_up, w_down, norm_final)


def _encoder(x, p, rope_tabs):
    b, s, dm = x.shape
    u, qkv1, qkv2, qkv3 = _in_proj(x, p["norm_mix"], p["w_in"], rope_tabs)
    attn = _attention(qkv1, qkv2, qkv3)
    x1 = _mix(x, u, attn, p["norm_mix"], p["pool_maps"], p["pool_scale"], p["w_up_pool"], p["w_up_attn"],
              p["w_gate"], p["b_gate"], p["w_out"])
    y = _ffn(x1.reshape(b * s, dm), p["norm_ffn"], p["w_ffn_gate"], p["w_ffn_up"], p["w_ffn_down"], p["norm_final"])
    return y.reshape(b, s, dm)


def kernel(x_prompt, x_sample, norm_mix, w_in, pool_maps, pool_scale, w_up_pool, w_up_attn, w_gate, b_gate,
           w_out, norm_ffn, w_ffn_gate, w_ffn_up, w_ffn_down, norm_final):
    assert norm_mix.shape[0] == 1, "one encoder layer"
    p = {
        "norm_mix": norm_mix[0][None, :],
        "w_in": w_in[0].astype(BF16),
        "pool_maps": pool_maps[0].astype(BF16),
        "pool_scale": pool_scale[0][None, :],
        "w_up_pool": w_up_pool[0].astype(BF16),
        "w_up_attn": w_up_attn[0].astype(BF16),
        "w_gate": w_gate[0].astype(BF16),
        "b_gate": b_gate[0][None, :],
        "w_out": w_out[0].astype(BF16),
        "norm_ffn": norm_ffn[0][None, :],
        "w_ffn_gate": w_ffn_gate[0].astype(BF16),
        "w_ffn_up": w_ffn_up[0].astype(BF16),
        "w_ffn_down": w_ffn_down[0].astype(BF16),
        "norm_final": norm_final[None, :],
    }
    outs = []
    for x in (x_prompt, x_sample):
        rope_tabs = _rope_tables(x.shape[1])
        outs.append(_encoder(x, p, rope_tabs))
    return tuple(outs)
```

```python
import functools

import jax
import jax.numpy as jnp
import numpy as np
from jax import lax
from jax.experimental import pallas as pl
from jax.experimental.pallas import tpu as pltpu

HEAD_DIM = 64
HEADS_PER_GROUP = 4
ATTN_PATTERNS = ((128, 1), (512, 4), (2048, 16))
N_GROUPS = len(ATTN_PATTERNS)
GROUP_WIDTH = HEADS_PER_GROUP * HEAD_DIM
ATTN_WIDTH = N_GROUPS * GROUP_WIDTH
ROPE_DIM = HEAD_DIM // 4
ROPE_THETA = 500000.0
POOL_WINDOWS = (2, 4, 8, 16)
POOL_GROUP = 128
POOL_WIDTH = POOL_GROUP * len(POOL_WINDOWS)
POOL_HALO = 8
NORM_EPS = 1e-6

LANES = 128
Q_BLOCK = 128
BAND = 64
KEY_SPAN = Q_BLOCK + 2 * BAND
PAIRS = GROUP_WIDTH // LANES
UNITS = 16
STAGED_UNITS = 4
TOKEN_TILE = 1024
IN_TILE = 1024
FFN_TILE = 1024
SUB_ROWS = 256
COL_CHUNK = 256
WIDE_CHUNK = 512
VMEM_LIMIT_BYTES = 56 * 1024 * 1024

F32 = jnp.float32
BF16 = jnp.bfloat16


def _rms_scale(x, gain):
    return x * lax.rsqrt(jnp.mean(x * x, axis=-1, keepdims=True) + NORM_EPS) * gain


def _rope_tables(seq):
    half = ROPE_DIM // 2
    inv = (ROPE_THETA ** (-np.arange(half, dtype=np.float32) / half)).astype(np.float32).astype(np.float64)
    ang = np.arange(seq, dtype=np.float64)[:, None] * inv[None, :]
    cos, sin = np.cos(ang), np.sin(ang)
    c = np.ones((seq, HEAD_DIM))
    s_up = np.zeros((seq, HEAD_DIM))
    s_down = np.zeros((seq, HEAD_DIM))
    c[:, :half] = cos
    c[:, half:ROPE_DIM] = cos
    s_down[:, :half] = -sin
    s_up[:, half:ROPE_DIM] = sin
    rep = LANES // HEAD_DIM
    return tuple(jnp.asarray(np.tile(t, (1, rep)), dtype=F32) for t in (c, s_up, s_down))


def _in_proj_kernel(x_ref, g_ref, w_ref, c_ref, su_ref, sd_ref, u_ref, qkv1_ref, qkv2_ref, qkv3_ref, perm_ref):
    tm = x_ref.shape[1]
    n_cols = w_ref.shape[1]
    assert POOL_WIDTH % GROUP_WIDTH == 0 and WIDE_CHUNK % GROUP_WIDTH == 0
    half = ROPE_DIM // 2
    scale = HEAD_DIM ** -0.5 * np.log2(np.e)
    chunks = 3 * GROUP_WIDTH // LANES

    def normed(r0):
        return _rms_scale(x_ref[0, r0:r0 + SUB_ROWS, :], g_ref[...]).astype(BF16)

    def emit_wide(r0, col, z):
        for off in range(0, z.shape[1], GROUP_WIDTH):
            emit(r0, col + off, z[:, off:off + GROUP_WIDTH])

    def emit(r0, col, z):
        rows = slice(r0, r0 + SUB_ROWS)
        if col < POOL_WIDTH:
            u_ref[0, rows, col:col + GROUP_WIDTH] = z
            return
        part, g = divmod((col - POOL_WIDTH) // GROUP_WIDTH, N_GROUPS)
        for j in range(GROUP_WIDTH // LANES):
            t = z[:, j * LANES:(j + 1) * LANES]
            if part < 2:
                t = (t * c_ref[rows, :] + pltpu.roll(t, half, 1) * su_ref[rows, :]
                     + pltpu.roll(t, LANES - half, 1) * sd_ref[rows, :])
            if part == 0:
                t = t * scale
            dst = part * GROUP_WIDTH + j * LANES
            if g == 0:
                qkv1_ref[0, rows, dst:dst + LANES] = t.astype(BF16)
            else:
                perm_ref[(g - 1) * chunks + dst // LANES, rows, :] = t

    def deinterleave(r0):
        for g, out_ref in ((1, qkv2_ref), (2, qkv3_ref)):
            d = ATTN_PATTERNS[g][1]
            n = SUB_ROWS // d
            for r in range(d):
                for ch in range(chunks):
                    rows = perm_ref[(g - 1) * chunks + ch, pl.ds(r0 + r, n, stride=d), :]
                    out_ref[0, r, r0 // d:r0 // d + n, ch * LANES:(ch + 1) * LANES] = rows.astype(BF16)

    h = normed(0)
    for r0 in range(0, tm, SUB_ROWS):
        pending = None
        for col in range(0, n_cols, WIDE_CHUNK):
            z = jnp.dot(h, w_ref[:, col:min(col + WIDE_CHUNK, n_cols)], preferred_element_type=F32)
            if pending is not None:
                emit_wide(r0, *pending)
            elif r0 > 0:
                deinterleave(r0 - SUB_ROWS)
            pending = (col, z)
        emit_wide(r0, *pending)
        if r0 + SUB_ROWS < tm:
            h = normed(r0 + SUB_ROWS)
    deinterleave(tm - SUB_ROWS)


def _in_proj(x, norm_g, w_in, rope_tabs):
    b, s, dm = x.shape
    tm = IN_TILE
    d2, d3 = ATTN_PATTERNS[1][1], ATTN_PATTERNS[2][1]
    qkv_w = 3 * GROUP_WIDTH
    const = lambda bi, i: (0, 0)
    tab_spec = pl.BlockSpec((tm, LANES), lambda bi, i: (i, 0))
    return pl.pallas_call(
        _in_proj_kernel,
        grid=(b, s // tm),
        in_specs=[
            pl.BlockSpec((1, tm, dm), lambda bi, i: (bi, i, 0)),
            pl.BlockSpec((1, dm), const),
            pl.BlockSpec(w_in.shape, const),
            tab_spec, tab_spec, tab_spec,
        ],
        out_specs=[
            pl.BlockSpec((1, tm, POOL_WIDTH), lambda bi, i: (bi, i, 0)),
            pl.BlockSpec((1, tm, qkv_w), lambda bi, i: (bi, i, 0)),
            pl.BlockSpec((1, d2, tm // d2, qkv_w), lambda bi, i: (bi, 0, i, 0)),
            pl.BlockSpec((1, d3, tm // d3, qkv_w), lambda bi, i: (bi, 0, i, 0)),
        ],
        out_shape=[
            jax.ShapeDtypeStruct((b, s, POOL_WIDTH), F32),
            jax.ShapeDtypeStruct((b, s, qkv_w), BF16),
            jax.ShapeDtypeStruct((b, d2, s // d2, qkv_w), BF16),
            jax.ShapeDtypeStruct((b, d3, s // d3, qkv_w), BF16),
        ],
        scratch_shapes=[pltpu.VMEM((2 * qkv_w // LANES, tm, LANES), F32)],
        compiler_params=pltpu.CompilerParams(
            dimension_semantics=("parallel", "parallel"), vmem_limit_bytes=VMEM_LIMIT_BYTES),
        name="in_proj",
    )(x, norm_g, w_in, *rope_tabs)


def _attend_chunk(g, units, nk, s_ref, p_ref, o_ref, m_ref, d_ref):
    low = lax.broadcasted_iota(jnp.int32, (Q_BLOCK, LANES), 1) < HEAD_DIM
    items = [(u, j) for u in range(len(units)) for j in range(PAIRS)]

    def scores(u, j):
        load = units[u][0]
        q2 = load(0, j)
        zero = jnp.zeros_like(q2)
        stacked = jnp.concatenate([jnp.where(low, q2, zero), jnp.where(low, zero, q2)], axis=0)
        s_ref[u, j, :, :nk] = lax.dot_general(stacked, load(1, j), (((1,), (1,)), ((), ())),
                                              preferred_element_type=F32)

    def softmax(u, j):
        _, limit, rows = units[u]
        tops, dens = [], []
        for head in range(2):
            hrows = slice(head * Q_BLOCK, (head + 1) * Q_BLOCK)
            s = jnp.minimum(s_ref[u, j, hrows, :nk], limit)
            m = jnp.max(s, axis=-1, keepdims=True)
            p = jnp.exp2(s - m)
            p_ref[u, j, hrows, :nk] = p.astype(BF16)
            tops.append(m)
            dens.append(jnp.sum(p, axis=-1, keepdims=True))
        m_ref[g, j, rows, :] = jnp.where(low, tops[0], tops[1])
        d_ref[g, j, rows, :] = jnp.where(low, dens[0], dens[1])

    def values(u, j):
        load, _, rows = units[u]
        o2 = jnp.dot(p_ref[u, j, :, :nk], load(2, j), preferred_element_type=F32)
        o_ref[g, j, rows, :] = jnp.where(low, o2[:Q_BLOCK], o2[Q_BLOCK:])

    if nk < KEY_SPAN:
        for stage in (scores, softmax, values):
            for item in items:
                stage(*item)
        return

    for step in range(len(items) + 2):
        if step < len(items):
            scores(*items[step])
        if 2 <= step:
            values(*items[step - 2])
        if 1 <= step <= len(items):
            softmax(*items[step - 1])


def _attn_kernel(qkv1_ref, qkv2_ref, qkv3_ref, out_ref, o_ref, m_ref, d_ref, s_ref, p_ref, lim_ref):
    seq = out_ref.shape[1]
    gw = GROUP_WIDTH
    scratch = (s_ref, p_ref, o_ref, m_ref, d_ref)

    rel = (lax.broadcasted_iota(jnp.int32, (Q_BLOCK, KEY_SPAN), 1)
           - lax.broadcasted_iota(jnp.int32, (Q_BLOCK, KEY_SPAN), 0))
    for variant in range(3):
        lim_ref[variant] = jnp.where(jnp.abs(rel - variant * BAND) <= BAND, jnp.inf, -jnp.inf)

    def limit_of(variant, nk):
        return lim_ref[variant, :, :nk]

    def cols(part, j):
        return slice(part * gw + j * LANES, part * gw + (j + 1) * LANES)

    def chunk0(c, carry):
        units = []
        for b in range(UNITS):
            blk = c * UNITS + b
            qs = pl.multiple_of(blk * Q_BLOCK, Q_BLOCK)
            variant = jnp.where(blk == 0, 0, jnp.where(blk == seq // Q_BLOCK - 1, 2, 1))
            ws = pl.multiple_of(qs - variant * BAND, BAND)

            def load(part, j, qs=qs, ws=ws):
                rows = pl.ds(qs, Q_BLOCK) if part == 0 else pl.ds(ws, KEY_SPAN)
                return qkv1_ref[0, rows, cols(part, j)]

            units.append((load, limit_of(variant, KEY_SPAN), pl.ds(qs, Q_BLOCK)))
        _attend_chunk(0, units, KEY_SPAN, *scratch)
        return carry

    lax.fori_loop(0, seq // (UNITS * Q_BLOCK), chunk0, 0)

    d2 = ATTN_PATTERNS[1][1]
    blocks2 = seq // d2 // Q_BLOCK
    assert UNITS % blocks2 == 0 and (d2 * blocks2) % UNITS == 0

    def chunk1(c, carry):
        units = []
        for b in range(UNITS):
            r = c * (UNITS // blocks2) + b // blocks2
            qs = (b % blocks2) * Q_BLOCK
            variant = 0 if qs == 0 else 2 if b % blocks2 == blocks2 - 1 else 1
            ws = qs - variant * BAND

            def load(part, j, r=r, qs=qs, ws=ws):
                rows = pl.ds(qs, Q_BLOCK) if part == 0 else pl.ds(ws, KEY_SPAN)
                return qkv2_ref[0, r, rows, cols(part, j)]

            units.append((load, limit_of(variant, KEY_SPAN), pl.ds(qs * d2 + r, Q_BLOCK, stride=d2)))
        _attend_chunk(1, units, KEY_SPAN, *scratch)
        return carry

    lax.fori_loop(0, d2 * blocks2 // UNITS, chunk1, 0)

    d3 = ATTN_PATTERNS[2][1]
    n3 = seq // d3

    def chunk2(c, carry):
        units = []
        for b in range(STAGED_UNITS):
            r = c * STAGED_UNITS + b
            units.append((lambda part, j, r=r: qkv3_ref[0, r, :, cols(part, j)], limit_of(0, n3),
                          pl.ds(r, n3, stride=d3)))
        _attend_chunk(2, units, n3, *scratch)
        return carry

    lax.fori_loop(0, d3 // STAGED_UNITS, chunk2, 0)

    def merge(i, carry):
        rows = pl.ds(pl.multiple_of(i * Q_BLOCK, Q_BLOCK), Q_BLOCK)
        for j in range(PAIRS):
            tops = [m_ref[g, j, rows, :] for g in range(N_GROUPS)]
            top = functools.reduce(jnp.maximum, tops)
            scales = [jnp.exp2(t - top) for t in tops]
            num = sum(e * o_ref[g, j, rows, :] for g, e in enumerate(scales))
            den = sum(e * d_ref[g, j, rows, :] for g, e in enumerate(scales))
            out_ref[0, rows, j * LANES:(j + 1) * LANES] = (num / den).astype(out_ref.dtype)
        return carry

    lax.fori_loop(0, seq // Q_BLOCK, merge, 0)


def _attention(qkv1, qkv2, qkv3):
    b, s, _ = qkv1.shape
    assert s // ATTN_PATTERNS[2][1] == Q_BLOCK and s % (UNITS * Q_BLOCK) == 0 and ATTN_PATTERNS[2][1] % STAGED_UNITS == 0
    assert all(w // 2 // d == BAND for w, d in ATTN_PATTERNS)
    return pl.pallas_call(
        _attn_kernel,
        grid=(b,),
        in_specs=[
            pl.BlockSpec((1,) + qkv1.shape[1:], lambda bi: (bi, 0, 0)),
            pl.BlockSpec((1,) + qkv2.shape[1:], lambda bi: (bi, 0, 0, 0)),
            pl.BlockSpec((1,) + qkv3.shape[1:], lambda bi: (bi, 0, 0, 0)),
        ],
        out_specs=pl.BlockSpec((1, s, GROUP_WIDTH), lambda bi: (bi, 0, 0)),
        out_shape=jax.ShapeDtypeStruct((b, s, GROUP_WIDTH), BF16),
        scratch_shapes=[
            pltpu.VMEM((N_GROUPS, PAIRS, s, LANES), F32),
            pltpu.VMEM((N_GROUPS, PAIRS, s, LANES), F32),
            pltpu.VMEM((N_GROUPS, PAIRS, s, LANES), F32),
            pltpu.VMEM((UNITS, PAIRS, 2 * Q_BLOCK, KEY_SPAN), F32),
            pltpu.VMEM((UNITS, PAIRS, 2 * Q_BLOCK, KEY_SPAN), BF16),
            pltpu.VMEM((3, Q_BLOCK, KEY_SPAN), F32),
        ],
        compiler_params=pltpu.CompilerParams(
            dimension_semantics=("parallel",), vmem_limit_bytes=VMEM_LIMIT_BYTES),
        name="attention",
    )(qkv1, qkv2, qkv3)


def _fold_pool_kernel(maps_ref, scale_ref, wup_ref, out_ref):
    for g in range(len(POOL_WINDOWS)):
        rows = slice(g * POOL_GROUP, (g + 1) * POOL_GROUP)
        out_ref[rows, :] = jnp.dot(maps_ref[g], scale_ref[rows, :] * wup_ref[rows, :], preferred_element_type=F32,
                                   precision=lax.Precision.HIGHEST).astype(out_ref.dtype)


def _fold_pool(maps, pool_scale, w_up_pool):
    return pl.pallas_call(
        _fold_pool_kernel,
        out_shape=jax.ShapeDtypeStruct(w_up_pool.shape, BF16),
        name="fold_pool",
    )(maps, pool_scale.reshape(POOL_WIDTH, 1), w_up_pool)


def _mix_kernel(x_ref, u_ref, up_ref, un_ref, at_ref, g_ref, wup_ref, wua_ref, wg_ref, bg_ref, wo_ref, out_ref,
                ext_ref):
    tm = x_ref.shape[1]
    dm = x_ref.shape[2]
    i = pl.program_id(1)
    seq = tm * pl.num_programs(1)

    ext_ref[0:POOL_HALO, :] = jnp.where(i > 0, up_ref[0], 0.0)
    ext_ref[POOL_HALO:POOL_HALO + tm, :] = u_ref[0]
    ext_ref[POOL_HALO + tm:, :] = jnp.where(i < pl.num_programs(1) - 1, un_ref[0], 0.0)

    n_chunks = dm // COL_CHUNK
    assert n_chunks == len(POOL_WINDOWS), "one pooling group per column chunk"

    def normed(r0):
        return _rms_scale(x_ref[0, r0:r0 + SUB_ROWS, :], g_ref[...]).astype(BF16)

    def pool_group(r0, j):
        w = POOL_WINDOWS[j]
        cols = slice(j * POOL_GROUP, (j + 1) * POOL_GROUP)
        pos = i * tm + r0 + lax.broadcasted_iota(jnp.int32, (SUB_ROWS, 1), 0)
        span = SUB_ROWS + 2 * POOL_HALO
        ext = ext_ref[r0:r0 + span, cols]
        shifted = lambda a, k: pltpu.roll(a, span - k, 0)
        run, width = ext, 1
        while 2 * width < w:
            run, width = run + shifted(run, width), 2 * width
        start = POOL_HALO - w // 2
        assert 2 * width == w and start + width == POOL_HALO
        lead = run if start == 0 else shifted(run, start)
        total = lead[:SUB_ROWS] + run[POOL_HALO:POOL_HALO + SUB_ROWS]
        cnt = (jnp.minimum(pos + w // 2, seq) - jnp.maximum(pos - w // 2, 0)).astype(F32)
        pooled = total / cnt - ext[POOL_HALO:POOL_HALO + SUB_ROWS]
        return pooled.astype(BF16)

    dot = functools.partial(jnp.dot, preferred_element_type=F32)

    def project(h, pooled, attn, c):
        ca, cb = slice(c, c + COL_CHUNK), slice(dm + c, dm + c + COL_CHUNK)
        return (dot(h, wg_ref[:, ca]) + bg_ref[:, ca], dot(h, wg_ref[:, cb]) + bg_ref[:, cb],
                dot(pooled, wup_ref[:, ca]), dot(attn, wua_ref[:, ca]))

    def gate(ga, gb, a, b):
        return (jax.nn.sigmoid(ga) * a + jax.nn.sigmoid(gb) * b).astype(BF16)

    h = normed(0)
    pooled = jnp.concatenate([pool_group(0, j) for j in range(n_chunks)], axis=-1)
    for r0 in range(0, tm, SUB_ROWS):
        more = r0 + SUB_ROWS < tm
        attn = at_ref[0, r0:r0 + SUB_ROWS, :]
        ys, pending, upcoming = [], None, []
        for j in range(n_chunks):
            parts = project(h, pooled, attn, j * COL_CHUNK)
            if pending is not None:
                ys.append(gate(*pending))
            pending = parts
            if more:
                upcoming.append(pool_group(r0 + SUB_ROWS, j))
        ys.append(gate(*pending))
        delta = dot(jnp.concatenate(ys, axis=-1), wo_ref[...])
        if more:
            h = normed(r0 + SUB_ROWS)
            pooled = jnp.concatenate(upcoming, axis=-1)
        out_ref[0, r0:r0 + SUB_ROWS, :] = x_ref[0, r0:r0 + SUB_ROWS, :] + delta


def _mix(x, u, attn, norm_g, w_up_pool, w_up_attn, w_gate, b_gate, w_out):
    b, s, dm = x.shape
    tm = TOKEN_TILE
    halo_blocks = tm // POOL_HALO
    last_halo = s // POOL_HALO - 1
    tile = lambda bi, i: (bi, i, 0)
    const2 = lambda bi, i: (0, 0)
    return pl.pallas_call(
        _mix_kernel,
        grid=(b, s // tm),
        in_specs=[
            pl.BlockSpec((1, tm, dm), tile),
            pl.BlockSpec((1, tm, POOL_WIDTH), tile),
            pl.BlockSpec((1, POOL_HALO, POOL_WIDTH), lambda bi, i: (bi, jnp.maximum(i * halo_blocks - 1, 0), 0)),
            pl.BlockSpec((1, POOL_HALO, POOL_WIDTH),
                         lambda bi, i: (bi, jnp.minimum((i + 1) * halo_blocks, last_halo), 0)),
            pl.BlockSpec((1, tm, GROUP_WIDTH), tile),
            pl.BlockSpec((1, dm), const2),
            pl.BlockSpec(w_up_pool.shape, const2),
            pl.BlockSpec(w_up_attn.shape, const2),
            pl.BlockSpec(w_gate.shape, const2),
            pl.BlockSpec((1, 2 * dm), const2),
            pl.BlockSpec(w_out.shape, const2),
        ],
        out_specs=pl.BlockSpec((1, tm, dm), tile),
        out_shape=jax.ShapeDtypeStruct((b, s, dm), F32),
        scratch_shapes=[pltpu.VMEM((tm + 2 * POOL_HALO, POOL_WIDTH), F32)],
        compiler_params=pltpu.CompilerParams(
            dimension_semantics=("parallel", "parallel"), vmem_limit_bytes=VMEM_LIMIT_BYTES),
        name="mix",
    )(x, u, u, u, attn, norm_g, w_up_pool, w_up_attn, w_gate, b_gate, w_out)


def _ffn_kernel(x_ref, g_ref, wg_ref, wu_ref, wd_ref, gf_ref, out_ref):
    tm = x_ref.shape[0]
    d_ff = wg_ref.shape[1]
    dot = functools.partial(jnp.dot, preferred_element_type=F32)

    def normed(r0):
        return _rms_scale(x_ref[r0:r0 + SUB_ROWS, :], g_ref[...]).astype(BF16)

    def swish_gate(parts):
        gate, up = parts
        return (gate * jax.nn.sigmoid(gate) * up).astype(BF16)

    def finish(r0, delta):
        out_ref[r0:r0 + SUB_ROWS, :] = _rms_scale(x_ref[r0:r0 + SUB_ROWS, :] + delta, gf_ref[...])

    h = normed(0)
    unfinished = None
    for r0 in range(0, tm, SUB_ROWS):
        acts, pending = [], None
        for c in range(0, d_ff, COL_CHUNK):
            cols = slice(c, c + COL_CHUNK)
            parts = (dot(h, wg_ref[:, cols]), dot(h, wu_ref[:, cols]))
            if pending is not None:
                acts.append(swish_gate(pending))
            elif unfinished is not None:
                finish(*unfinished)
            pending = parts
        acts.append(swish_gate(pending))
        if r0 + SUB_ROWS < tm:
            h = normed(r0 + SUB_ROWS)
        unfinished = (r0, dot(jnp.concatenate(acts, axis=-1), wd_ref[...]))
    finish(*unfinished)


def _ffn(x, norm_g, w_gate, w_up, w_down, norm_final):
    n, dm = x.shape
    tm = FFN_TILE
    const = lambda i: (0, 0)
    resident = functools.partial(pl.BlockSpec, index_map=const, pipeline_mode=pl.Buffered(1))
    return pl.pallas_call(
        _ffn_kernel,
        grid=(n // tm,),
        in_specs=[
            pl.BlockSpec((tm, dm), lambda i: (i, 0)),
            pl.BlockSpec((1, dm), const),
            resident(w_gate.shape),
            resident(w_up.shape),
            resident(w_down.shape),
            pl.BlockSpec((1, dm), const),
        ],
        out_specs=pl.BlockSpec((tm, dm), lambda i: (i, 0)),
        out_shape=jax.ShapeDtypeStruct((n, dm), F32),
        compiler_params=pltpu.CompilerParams(
            dimension_semantics=("parallel",), vmem_limit_bytes=VMEM_LIMIT_BYTES),
        name="ffn",
    )(x, norm_g, w_gate, w_up, w_down, norm_final)


def _encoder(x, p, rope_tabs):
    b, s, dm = x.shape
    u, qkv1, qkv2, qkv3 = _in_proj(x, p["norm_mix"], p["w_in"], rope_tabs)
    attn = _attention(qkv1, qkv2, qkv3)
    x1 = _mix(x, u, attn, p["norm_mix"], p["w_pool"], p["w_up_attn"], p["w_gate"], p["b_gate"], p["w_out"])
    y = _ffn(x1.reshape(b * s, dm), p["norm_ffn"], p["w_ffn_gate"], p["w_ffn_up"], p["w_ffn_down"], p["norm_final"])
    return y.reshape(b, s, dm)


def kernel(x_prompt, x_sample, norm_mix, w_in, pool_maps, pool_scale, w_up_pool, w_up_attn, w_gate, b_gate,
           w_out, norm_ffn, w_ffn_gate, w_ffn_up, w_ffn_down, norm_final):
    assert norm_mix.shape[0] == 1, "one encoder layer"
    p = {
        "norm_mix": norm_mix[0][None, :],
        "w_in": w_in[0].astype(BF16),
        "w_pool": _fold_pool(pool_maps[0], pool_scale[0], w_up_pool[0]),
        "w_up_attn": w_up_attn[0].astype(BF16),
        "w_gate": w_gate[0].astype(BF16),
        "b_gate": b_gate[0][None, :],
        "w_out": w_out[0].astype(BF16),
        "norm_ffn": norm_ffn[0][None, :],
        "w_ffn_gate": w_ffn_gate[0].astype(BF16),
        "w_ffn_up": w_ffn_up[0].astype(BF16),
        "w_ffn_down": w_ffn_down[0].astype(BF16),
        "norm_final": norm_final[None, :],
    }
    outs = []
    for x in (x_prompt, x_sample):
        rope_tabs = _rope_tables(x.shape[1])
        outs.append(_encoder(x, p, rope_tabs))
    return tuple(outs)
```

```python
import functools

import jax
import jax.numpy as jnp
import numpy as np
from jax import lax
from jax.experimental import pallas as pl
from jax.experimental.pallas import tpu as pltpu

HEAD_DIM = 64
HEADS_PER_GROUP = 4
ATTN_PATTERNS = ((128, 1), (512, 4), (2048, 16))
N_GROUPS = len(ATTN_PATTERNS)
GROUP_WIDTH = HEADS_PER_GROUP * HEAD_DIM
ATTN_WIDTH = N_GROUPS * GROUP_WIDTH
ROPE_DIM = HEAD_DIM // 4
ROPE_THETA = 500000.0
POOL_WINDOWS = (2, 4, 8, 16)
POOL_GROUP = 128
POOL_WIDTH = POOL_GROUP * len(POOL_WINDOWS)
POOL_HALO = 8
NORM_EPS = 1e-6

LANES = 128
Q_BLOCK = 128
BAND = 64
KEY_SPAN = Q_BLOCK + 2 * BAND
PAIRS = GROUP_WIDTH // LANES
UNITS = 16
STAGED_UNITS = 4
TOKEN_TILE = 1024
IN_TILE = 1024
FFN_TILE = 1024
SUB_ROWS = 256
COL_CHUNK = 256
WIDE_CHUNK = 512
VMEM_LIMIT_BYTES = 56 * 1024 * 1024

F32 = jnp.float32
BF16 = jnp.bfloat16


def _rms_scale(x, gain):
    return x * lax.rsqrt(jnp.mean(x * x, axis=-1, keepdims=True) + NORM_EPS) * gain


def _rope_tables(seq):
    half = ROPE_DIM // 2
    inv = (ROPE_THETA ** (-np.arange(half, dtype=np.float32) / half)).astype(np.float32).astype(np.float64)
    ang = np.arange(seq, dtype=np.float64)[:, None] * inv[None, :]
    cos, sin = np.cos(ang), np.sin(ang)
    c = np.ones((seq, HEAD_DIM))
    s_up = np.zeros((seq, HEAD_DIM))
    s_down = np.zeros((seq, HEAD_DIM))
    c[:, :half] = cos
    c[:, half:ROPE_DIM] = cos
    s_down[:, :half] = -sin
    s_up[:, half:ROPE_DIM] = sin
    rep = LANES // HEAD_DIM
    return tuple(jnp.asarray(np.tile(t, (1, rep)), dtype=F32) for t in (c, s_up, s_down))


def _in_proj_kernel(x_ref, g_ref, w_ref, c_ref, su_ref, sd_ref, u_ref, qkv1_ref, qkv2_ref, qkv3_ref, perm_ref):
    tm = x_ref.shape[1]
    n_cols = w_ref.shape[1]
    assert POOL_WIDTH % GROUP_WIDTH == 0 and WIDE_CHUNK % GROUP_WIDTH == 0
    half = ROPE_DIM // 2
    scale = HEAD_DIM ** -0.5 * np.log2(np.e)
    chunks = 3 * GROUP_WIDTH // LANES

    def normed(r0):
        return _rms_scale(x_ref[0, r0:r0 + SUB_ROWS, :], g_ref[...]).astype(BF16)

    def emit_wide(r0, col, z):
        for off in range(0, z.shape[1], GROUP_WIDTH):
            emit(r0, col + off, z[:, off:off + GROUP_WIDTH])

    def emit(r0, col, z):
        rows = slice(r0, r0 + SUB_ROWS)
        if col < POOL_WIDTH:
            u_ref[0, rows, col:col + GROUP_WIDTH] = z
            return
        part, g = divmod((col - POOL_WIDTH) // GROUP_WIDTH, N_GROUPS)
        for j in range(GROUP_WIDTH // LANES):
            t = z[:, j * LANES:(j + 1) * LANES]
            if part < 2:
                t = (t * c_ref[rows, :] + pltpu.roll(t, half, 1) * su_ref[rows, :]
                     + pltpu.roll(t, LANES - half, 1) * sd_ref[rows, :])
            if part == 0:
                t = t * scale
            dst = part * GROUP_WIDTH + j * LANES
            if g == 0:
                qkv1_ref[0, rows, dst:dst + LANES] = t.astype(BF16)
            else:
                perm_ref[(g - 1) * chunks + dst // LANES, rows, :] = t

    def deinterleave(r0):
        for g, out_ref in ((1, qkv2_ref), (2, qkv3_ref)):
            d = ATTN_PATTERNS[g][1]
            n = SUB_ROWS // d
            for r in range(d):
                for ch in range(chunks):
                    rows = perm_ref[(g - 1) * chunks + ch, pl.ds(r0 + r, n, stride=d), :]
                    out_ref[0, r, r0 // d:r0 // d + n, ch * LANES:(ch + 1) * LANES] = rows.astype(BF16)

    h = normed(0)
    for r0 in range(0, tm, SUB_ROWS):
        pending = None
        for col in range(0, n_cols, WIDE_CHUNK):
            z = jnp.dot(h, w_ref[:, col:min(col + WIDE_CHUNK, n_cols)], preferred_element_type=F32)
            if pending is not None:
                emit_wide(r0, *pending)
            elif r0 > 0:
                deinterleave(r0 - SUB_ROWS)
            pending = (col, z)
        emit_wide(r0, *pending)
        if r0 + SUB_ROWS < tm:
            h = normed(r0 + SUB_ROWS)
    deinterleave(tm - SUB_ROWS)


def _in_proj(x, norm_g, w_in, rope_tabs):
    b, s, dm = x.shape
    tm = IN_TILE
    d2, d3 = ATTN_PATTERNS[1][1], ATTN_PATTERNS[2][1]
    qkv_w = 3 * GROUP_WIDTH
    const = lambda bi, i: (0, 0)
    tab_spec = pl.BlockSpec((tm, LANES), lambda bi, i: (i, 0))
    return pl.pallas_call(
        _in_proj_kernel,
        grid=(b, s // tm),
        in_specs=[
            pl.BlockSpec((1, tm, dm), lambda bi, i: (bi, i, 0)),
            pl.BlockSpec((1, dm), const),
            pl.BlockSpec(w_in.shape, const),
            tab_spec, tab_spec, tab_spec,
        ],
        out_specs=[
            pl.BlockSpec((1, tm, POOL_WIDTH), lambda bi, i: (bi, i, 0)),
            pl.BlockSpec((1, tm, qkv_w), lambda bi, i: (bi, i, 0)),
            pl.BlockSpec((1, d2, tm // d2, qkv_w), lambda bi, i: (bi, 0, i, 0)),
            pl.BlockSpec((1, d3, tm // d3, qkv_w), lambda bi, i: (bi, 0, i, 0)),
        ],
        out_shape=[
            jax.ShapeDtypeStruct((b, s, POOL_WIDTH), F32),
            jax.ShapeDtypeStruct((b, s, qkv_w), BF16),
            jax.ShapeDtypeStruct((b, d2, s // d2, qkv_w), BF16),
            jax.ShapeDtypeStruct((b, d3, s // d3, qkv_w), BF16),
        ],
        scratch_shapes=[pltpu.VMEM((2 * qkv_w // LANES, tm, LANES), F32)],
        compiler_params=pltpu.CompilerParams(
            dimension_semantics=("parallel", "parallel"), vmem_limit_bytes=VMEM_LIMIT_BYTES),
        name="in_proj",
    )(x, norm_g, w_in, *rope_tabs)


def _attend_chunk(g, units, nk, s_ref, p_ref, o_ref, m_ref, d_ref):
    low = lax.broadcasted_iota(jnp.int32, (Q_BLOCK, LANES), 1) < HEAD_DIM
    items = [(u, j) for u in range(len(units)) for j in range(PAIRS)]

    def scores(u, j):
        load = units[u][0]
        q2 = load(0, j)
        zero = jnp.zeros_like(q2)
        stacked = jnp.concatenate([jnp.where(low, q2, zero), jnp.where(low, zero, q2)], axis=0)
        s_ref[u, j, :, :nk] = lax.dot_general(stacked, load(1, j), (((1,), (1,)), ((), ())),
                                              preferred_element_type=F32)

    def softmax(u, j):
        _, limit, rows = units[u]
        tops, dens = [], []
        for head in range(2):
            hrows = slice(head * Q_BLOCK, (head + 1) * Q_BLOCK)
            s = jnp.minimum(s_ref[u, j, hrows, :nk], limit)
            m = jnp.max(s, axis=-1, keepdims=True)
            p = jnp.exp2(s - m)
            p_ref[u, j, hrows, :nk] = p.astype(BF16)
            tops.append(m)
            dens.append(jnp.sum(p, axis=-1, keepdims=True))
        m_ref[g, j, rows, :] = jnp.where(low, tops[0], tops[1])
        d_ref[g, j, rows, :] = jnp.where(low, dens[0], dens[1])

    def values(u, j):
        load, _, rows = units[u]
        o2 = jnp.dot(p_ref[u, j, :, :nk], load(2, j), preferred_element_type=F32)
        o_ref[g, j, rows, :] = jnp.where(low, o2[:Q_BLOCK], o2[Q_BLOCK:])

    if nk < KEY_SPAN:
        for stage in (scores, softmax, values):
            for item in items:
                stage(*item)
        return

    for step in range(len(items) + 2):
        if step < len(items):
            scores(*items[step])
        if 2 <= step:
            values(*items[step - 2])
        if 1 <= step <= len(items):
            softmax(*items[step - 1])


def _attn_kernel(qkv1_ref, qkv2_ref, qkv3_ref, out_ref, o_ref, m_ref, d_ref, s_ref, p_ref, lim_ref):
    seq = out_ref.shape[1]
    gw = GROUP_WIDTH
    scratch = (s_ref, p_ref, o_ref, m_ref, d_ref)

    rel = (lax.broadcasted_iota(jnp.int32, (Q_BLOCK, KEY_SPAN), 1)
           - lax.broadcasted_iota(jnp.int32, (Q_BLOCK, KEY_SPAN), 0))
    for variant in range(3):
        lim_ref[variant] = jnp.where(jnp.abs(rel - variant * BAND) <= BAND, jnp.inf, -jnp.inf)

    def limit_of(variant, nk):
        return lim_ref[variant, :, :nk]

    def cols(part, j):
        return slice(part * gw + j * LANES, part * gw + (j + 1) * LANES)

    def chunk0(c, carry):
        units = []
        for b in range(UNITS):
            blk = c * UNITS + b
            qs = pl.multiple_of(blk * Q_BLOCK, Q_BLOCK)
            variant = jnp.where(blk == 0, 0, jnp.where(blk == seq // Q_BLOCK - 1, 2, 1))
            ws = pl.multiple_of(qs - variant * BAND, BAND)

            def load(part, j, qs=qs, ws=ws):
                rows = pl.ds(qs, Q_BLOCK) if part == 0 else pl.ds(ws, KEY_SPAN)
                return qkv1_ref[0, rows, cols(part, j)]

            units.append((load, limit_of(variant, KEY_SPAN), pl.ds(qs, Q_BLOCK)))
        _attend_chunk(0, units, KEY_SPAN, *scratch)
        return carry

    lax.fori_loop(0, seq // (UNITS * Q_BLOCK), chunk0, 0)

    d2 = ATTN_PATTERNS[1][1]
    blocks2 = seq // d2 // Q_BLOCK
    assert UNITS % blocks2 == 0 and (d2 * blocks2) % UNITS == 0

    def chunk1(c, carry):
        units = []
        for b in range(UNITS):
            r = c * (UNITS // blocks2) + b // blocks2
            qs = (b % blocks2) * Q_BLOCK
            variant = 0 if qs == 0 else 2 if b % blocks2 == blocks2 - 1 else 1
            ws = qs - variant * BAND

            def load(part, j, r=r, qs=qs, ws=ws):
                rows = pl.ds(qs, Q_BLOCK) if part == 0 else pl.ds(ws, KEY_SPAN)
                return qkv2_ref[0, r, rows, cols(part, j)]

            units.append((load, limit_of(variant, KEY_SPAN), pl.ds(qs * d2 + r, Q_BLOCK, stride=d2)))
        _attend_chunk(1, units, KEY_SPAN, *scratch)
        return carry

    lax.fori_loop(0, d2 * blocks2 // UNITS, chunk1, 0)

    d3 = ATTN_PATTERNS[2][1]
    n3 = seq // d3

    def chunk2(c, carry):
        units = []
        for b in range(STAGED_UNITS):
            r = c * STAGED_UNITS + b
            units.append((lambda part, j, r=r: qkv3_ref[0, r, :, cols(part, j)], limit_of(0, n3),
                          pl.ds(r, n3, stride=d3)))
        _attend_chunk(2, units, n3, *scratch)
        return carry

    lax.fori_loop(0, d3 // STAGED_UNITS, chunk2, 0)

    def merge(i, carry):
        rows = pl.ds(pl.multiple_of(i * Q_BLOCK, Q_BLOCK), Q_BLOCK)
        for j in range(PAIRS):
            tops = [m_ref[g, j, rows, :] for g in range(N_GROUPS)]
            top = functools.reduce(jnp.maximum, tops)
            scales = [jnp.exp2(t - top) for t in tops]
            num = sum(e * o_ref[g, j, rows, :] for g, e in enumerate(scales))
            den = sum(e * d_ref[g, j, rows, :] for g, e in enumerate(scales))
            out_ref[0, rows, j * LANES:(j + 1) * LANES] = (num / den).astype(out_ref.dtype)
        return carry

    lax.fori_loop(0, seq // Q_BLOCK, merge, 0)


def _attention(qkv1, qkv2, qkv3):
    b, s, _ = qkv1.shape
    assert s // ATTN_PATTERNS[2][1] == Q_BLOCK and s % (UNITS * Q_BLOCK) == 0 and ATTN_PATTERNS[2][1] % STAGED_UNITS == 0
    assert all(w // 2 // d == BAND for w, d in ATTN_PATTERNS)
    return pl.pallas_call(
        _attn_kernel,
        grid=(b,),
        in_specs=[
            pl.BlockSpec((1,) + qkv1.shape[1:], lambda bi: (bi, 0, 0)),
            pl.BlockSpec((1,) + qkv2.shape[1:], lambda bi: (bi, 0, 0, 0)),
            pl.BlockSpec((1,) + qkv3.shape[1:], lambda bi: (bi, 0, 0, 0)),
        ],
        out_specs=pl.BlockSpec((1, s, GROUP_WIDTH), lambda bi: (bi, 0, 0)),
        out_shape=jax.ShapeDtypeStruct((b, s, GROUP_WIDTH), BF16),
        scratch_shapes=[
            pltpu.VMEM((N_GROUPS, PAIRS, s, LANES), F32),
            pltpu.VMEM((N_GROUPS, PAIRS, s, LANES), F32),
            pltpu.VMEM((N_GROUPS, PAIRS, s, LANES), F32),
            pltpu.VMEM((UNITS, PAIRS, 2 * Q_BLOCK, KEY_SPAN), F32),
            pltpu.VMEM((UNITS, PAIRS, 2 * Q_BLOCK, KEY_SPAN), BF16),
            pltpu.VMEM((3, Q_BLOCK, KEY_SPAN), F32),
        ],
        compiler_params=pltpu.CompilerParams(
            dimension_semantics=("parallel",), vmem_limit_bytes=VMEM_LIMIT_BYTES),
        name="attention",
    )(qkv1, qkv2, qkv3)


def _fold_pool_kernel(maps_ref, scale_ref, wup_ref, out_ref):
    for g in range(len(POOL_WINDOWS)):
        rows = slice(g * POOL_GROUP, (g + 1) * POOL_GROUP)
        out_ref[rows, :] = jnp.dot(maps_ref[g], scale_ref[rows, :] * wup_ref[rows, :], preferred_element_type=F32,
                                   precision=lax.Precision.HIGHEST).astype(out_ref.dtype)


def _fold_pool(maps, pool_scale, w_up_pool):
    return pl.pallas_call(
        _fold_pool_kernel,
        out_shape=jax.ShapeDtypeStruct(w_up_pool.shape, BF16),
        name="fold_pool",
    )(maps, pool_scale.reshape(POOL_WIDTH, 1), w_up_pool)


def _mix_kernel(x_ref, u_ref, up_ref, un_ref, at_ref, ic_ref, g_ref, wup_ref, wua_ref, wg_ref, bg_ref, wo_ref,
                out_ref, ext_ref):
    tm = x_ref.shape[1]
    dm = x_ref.shape[2]
    i = pl.program_id(1)
    ext_ref[0:POOL_HALO, :] = jnp.where(i > 0, up_ref[0], 0.0)
    ext_ref[POOL_HALO:POOL_HALO + tm, :] = u_ref[0]
    ext_ref[POOL_HALO + tm:, :] = jnp.where(i < pl.num_programs(1) - 1, un_ref[0], 0.0)

    n_chunks = dm // COL_CHUNK
    assert n_chunks == len(POOL_WINDOWS), "one pooling group per column chunk"

    def normed(r0):
        return _rms_scale(x_ref[0, r0:r0 + SUB_ROWS, :], g_ref[...]).astype(BF16)

    def pool_group(r0, j):
        w = POOL_WINDOWS[j]
        cols = slice(j * POOL_GROUP, (j + 1) * POOL_GROUP)
        span = SUB_ROWS + 2 * POOL_HALO
        ext = ext_ref[r0:r0 + span, cols]
        shifted = lambda a, k: pltpu.roll(a, span - k, 0)
        run, width = ext, 1
        while 2 * width < w:
            run, width = run + shifted(run, width), 2 * width
        start = POOL_HALO - w // 2
        assert 2 * width == w and start + width == POOL_HALO
        lead = run if start == 0 else shifted(run, start)
        total = lead[:SUB_ROWS] + run[POOL_HALO:POOL_HALO + SUB_ROWS]
        pooled = total * ic_ref[r0:r0 + SUB_ROWS, cols] - ext[POOL_HALO:POOL_HALO + SUB_ROWS]
        return pooled.astype(BF16)

    dot = functools.partial(jnp.dot, preferred_element_type=F32)

    def project(h, pooled, attn, c):
        ca, cb = slice(c, c + COL_CHUNK), slice(dm + c, dm + c + COL_CHUNK)
        return (dot(h, wg_ref[:, ca]) + bg_ref[:, ca], dot(h, wg_ref[:, cb]) + bg_ref[:, cb],
                dot(pooled, wup_ref[:, ca]), dot(attn, wua_ref[:, ca]))

    def gate(ga, gb, a, b):
        return (jax.nn.sigmoid(ga) * a + jax.nn.sigmoid(gb) * b).astype(BF16)

    h = normed(0)
    pooled = jnp.concatenate([pool_group(0, j) for j in range(n_chunks)], axis=-1)
    for r0 in range(0, tm, SUB_ROWS):
        more = r0 + SUB_ROWS < tm
        attn = at_ref[0, r0:r0 + SUB_ROWS, :]
        ys, pending, upcoming = [], None, []
        for j in range(n_chunks):
            parts = project(h, pooled, attn, j * COL_CHUNK)
            if pending is not None:
                ys.append(gate(*pending))
            pending = parts
            if more:
                upcoming.append(pool_group(r0 + SUB_ROWS, j))
        ys.append(gate(*pending))
        delta = dot(jnp.concatenate(ys, axis=-1), wo_ref[...])
        if more:
            h = normed(r0 + SUB_ROWS)
            pooled = jnp.concatenate(upcoming, axis=-1)
        out_ref[0, r0:r0 + SUB_ROWS, :] = x_ref[0, r0:r0 + SUB_ROWS, :] + delta


def _inverse_counts(seq):
    pos = np.arange(seq)
    cols = [1.0 / (np.minimum(pos + w // 2, seq) - np.maximum(pos - w // 2, 0)) for w in POOL_WINDOWS]
    return jnp.asarray(np.repeat(np.stack(cols, axis=1), POOL_GROUP, axis=1), dtype=F32)


def _mix(x, u, attn, norm_g, w_up_pool, w_up_attn, w_gate, b_gate, w_out):
    b, s, dm = x.shape
    tm = TOKEN_TILE
    halo_blocks = tm // POOL_HALO
    last_halo = s // POOL_HALO - 1
    tile = lambda bi, i: (bi, i, 0)
    const2 = lambda bi, i: (0, 0)
    return pl.pallas_call(
        _mix_kernel,
        grid=(b, s // tm),
        in_specs=[
            pl.BlockSpec((1, tm, dm), tile),
            pl.BlockSpec((1, tm, POOL_WIDTH), tile),
            pl.BlockSpec((1, POOL_HALO, POOL_WIDTH), lambda bi, i: (bi, jnp.maximum(i * halo_blocks - 1, 0), 0)),
            pl.BlockSpec((1, POOL_HALO, POOL_WIDTH),
                         lambda bi, i: (bi, jnp.minimum((i + 1) * halo_blocks, last_halo), 0)),
            pl.BlockSpec((1, tm, GROUP_WIDTH), tile),
            pl.BlockSpec((tm, POOL_WIDTH), lambda bi, i: (i, 0)),
            pl.BlockSpec((1, dm), const2),
            pl.BlockSpec(w_up_pool.shape, const2),
            pl.BlockSpec(w_up_attn.shape, const2),
            pl.BlockSpec(w_gate.shape, const2),
            pl.BlockSpec((1, 2 * dm), const2),
            pl.BlockSpec(w_out.shape, const2),
        ],
        out_specs=pl.BlockSpec((1, tm, dm), tile),
        out_shape=jax.ShapeDtypeStruct((b, s, dm), F32),
        scratch_shapes=[pltpu.VMEM((tm + 2 * POOL_HALO, POOL_WIDTH), F32)],
        compiler_params=pltpu.CompilerParams(
            dimension_semantics=("parallel", "parallel"), vmem_limit_bytes=VMEM_LIMIT_BYTES),
        name="mix",
    )(x, u, u, u, attn, _inverse_counts(s), norm_g, w_up_pool, w_up_attn, w_gate, b_gate, w_out)


def _ffn_kernel(x_ref, g_ref, wg_ref, wu_ref, wd_ref, gf_ref, out_ref):
    tm = x_ref.shape[0]
    d_ff = wg_ref.shape[1]
    dot = functools.partial(jnp.dot, preferred_element_type=F32)

    def normed(r0):
        return _rms_scale(x_ref[r0:r0 + SUB_ROWS, :], g_ref[...]).astype(BF16)

    def swish_gate(parts):
        gate, up = parts
        return (gate * jax.nn.sigmoid(gate) * up).astype(BF16)

    def finish(r0, delta):
        out_ref[r0:r0 + SUB_ROWS, :] = _rms_scale(x_ref[r0:r0 + SUB_ROWS, :] + delta, gf_ref[...])

    h = normed(0)
    unfinished = None
    for r0 in range(0, tm, SUB_ROWS):
        acts, pending = [], None
        for c in range(0, d_ff, COL_CHUNK):
            cols = slice(c, c + COL_CHUNK)
            parts = (dot(h, wg_ref[:, cols]), dot(h, wu_ref[:, cols]))
            if pending is not None:
                acts.append(swish_gate(pending))
            elif unfinished is not None:
                finish(*unfinished)
            pending = parts
        acts.append(swish_gate(pending))
        if r0 + SUB_ROWS < tm:
            h = normed(r0 + SUB_ROWS)
        unfinished = (r0, dot(jnp.concatenate(acts, axis=-1), wd_ref[...]))
    finish(*unfinished)


def _ffn(x, norm_g, w_gate, w_up, w_down, norm_final):
    n, dm = x.shape
    tm = FFN_TILE
    const = lambda i: (0, 0)
    resident = functools.partial(pl.BlockSpec, index_map=const, pipeline_mode=pl.Buffered(1))
    return pl.pallas_call(
        _ffn_kernel,
        grid=(n // tm,),
        in_specs=[
            pl.BlockSpec((tm, dm), lambda i: (i, 0)),
            pl.BlockSpec((1, dm), const),
            resident(w_gate.shape),
            resident(w_up.shape),
            resident(w_down.shape),
            pl.BlockSpec((1, dm), const),
        ],
        out_specs=pl.BlockSpec((tm, dm), lambda i: (i, 0)),
        out_shape=jax.ShapeDtypeStruct((n, dm), F32),
        compiler_params=pltpu.CompilerParams(
            dimension_semantics=("parallel",), vmem_limit_bytes=VMEM_LIMIT_BYTES),
        name="ffn",
    )(x, norm_g, w_gate, w_up, w_down, norm_final)


def _encoder(x, p, rope_tabs):
    b, s, dm = x.shape
    u, qkv1, qkv2, qkv3 = _in_proj(x, p["norm_mix"], p["w_in"], rope_tabs)
    attn = _attention(qkv1, qkv2, qkv3)
    x1 = _mix(x, u, attn, p["norm_mix"], p["w_pool"], p["w_up_attn"], p["w_gate"], p["b_gate"], p["w_out"])
    y = _ffn(x1.reshape(b * s, dm), p["norm_ffn"], p["w_ffn_gate"], p["w_ffn_up"], p["w_ffn_down"], p["norm_final"])
    return y.reshape(b, s, dm)


def kernel(x_prompt, x_sample, norm_mix, w_in, pool_maps, pool_scale, w_up_pool, w_up_attn, w_gate, b_gate,
           w_out, norm_ffn, w_ffn_gate, w_ffn_up, w_ffn_down, norm_final):
    assert norm_mix.shape[0] == 1, "one encoder layer"
    p = {
        "norm_mix": norm_mix[0][None, :],
        "w_in": w_in[0].astype(BF16),
        "w_pool": _fold_pool(pool_maps[0], pool_scale[0], w_up_pool[0]),
        "w_up_attn": w_up_attn[0].astype(BF16),
        "w_gate": w_gate[0].astype(BF16),
        "b_gate": b_gate[0][None, :],
        "w_out": w_out[0].astype(BF16),
        "norm_ffn": norm_ffn[0][None, :],
        "w_ffn_gate": w_ffn_gate[0].astype(BF16),
        "w_ffn_up": w_ffn_up[0].astype(BF16),
        "w_ffn_down": w_ffn_down[0].astype(BF16),
        "norm_final": norm_final[None, :],
    }
    outs = []
    for x in (x_prompt, x_sample):
        rope_tabs = _rope_tables(x.shape[1])
        outs.append(_encoder(x, p, rope_tabs))
    return tuple(outs)
```

```python
import functools

import jax
import jax.numpy as jnp
import numpy as np
from jax import lax
from jax.experimental import pallas as pl
from jax.experimental.pallas import tpu as pltpu

HEAD_DIM = 64
HEADS_PER_GROUP = 4
ATTN_PATTERNS = ((128, 1), (512, 4), (2048, 16))
N_GROUPS = len(ATTN_PATTERNS)
GROUP_WIDTH = HEADS_PER_GROUP * HEAD_DIM
ATTN_WIDTH = N_GROUPS * GROUP_WIDTH
ROPE_DIM = HEAD_DIM // 4
ROPE_THETA = 500000.0
POOL_WINDOWS = (2, 4, 8, 16)
POOL_GROUP = 128
POOL_WIDTH = POOL_GROUP * len(POOL_WINDOWS)
POOL_HALO = 8
NORM_EPS = 1e-6

LANES = 128
Q_BLOCK = 128
BAND = 64
KEY_SPAN = Q_BLOCK + 2 * BAND
PAIRS = GROUP_WIDTH // LANES
UNITS = 16
STAGED_UNITS = 4
TOKEN_TILE = 1024
IN_TILE = 1024
FFN_TILE = 1024
SUB_ROWS = 256
COL_CHUNK = 256
WIDE_CHUNK = 512
VMEM_LIMIT_BYTES = 56 * 1024 * 1024

F32 = jnp.float32
BF16 = jnp.bfloat16


def _rms_scale(x, gain):
    return x * lax.rsqrt(jnp.mean(x * x, axis=-1, keepdims=True) + NORM_EPS) * gain


def _rope_tables(seq):
    half = ROPE_DIM // 2
    inv = (ROPE_THETA ** (-np.arange(half, dtype=np.float32) / half)).astype(np.float32).astype(np.float64)
    ang = np.arange(seq, dtype=np.float64)[:, None] * inv[None, :]
    cos, sin = np.cos(ang), np.sin(ang)
    c = np.ones((seq, HEAD_DIM))
    s_up = np.zeros((seq, HEAD_DIM))
    s_down = np.zeros((seq, HEAD_DIM))
    c[:, :half] = cos
    c[:, half:ROPE_DIM] = cos
    s_down[:, :half] = -sin
    s_up[:, half:ROPE_DIM] = sin
    rep = LANES // HEAD_DIM
    return tuple(jnp.asarray(np.tile(t, (1, rep)), dtype=F32) for t in (c, s_up, s_down))


def _in_proj_kernel(x_ref, g_ref, w_ref, c_ref, su_ref, sd_ref, u_ref, qkv1_ref, qkv2_ref, qkv3_ref, perm_ref):
    tm = x_ref.shape[1]
    n_cols = w_ref.shape[1]
    assert POOL_WIDTH % GROUP_WIDTH == 0 and WIDE_CHUNK % GROUP_WIDTH == 0
    half = ROPE_DIM // 2
    scale = HEAD_DIM ** -0.5 * np.log2(np.e)
    chunks = 3 * GROUP_WIDTH // LANES

    def normed(r0):
        return _rms_scale(x_ref[0, r0:r0 + SUB_ROWS, :], g_ref[...]).astype(BF16)

    def emit_wide(r0, col, z):
        for off in range(0, z.shape[1], GROUP_WIDTH):
            emit(r0, col + off, z[:, off:off + GROUP_WIDTH])

    def emit(r0, col, z):
        rows = slice(r0, r0 + SUB_ROWS)
        if col < POOL_WIDTH:
            u_ref[0, rows, col:col + GROUP_WIDTH] = z
            return
        part, g = divmod((col - POOL_WIDTH) // GROUP_WIDTH, N_GROUPS)
        for j in range(GROUP_WIDTH // LANES):
            t = z[:, j * LANES:(j + 1) * LANES]
            if part < 2:
                t = (t * c_ref[rows, :] + pltpu.roll(t, half, 1) * su_ref[rows, :]
                     + pltpu.roll(t, LANES - half, 1) * sd_ref[rows, :])
            if part == 0:
                t = t * scale
            dst = part * GROUP_WIDTH + j * LANES
            if g == 0:
                qkv1_ref[0, rows, dst:dst + LANES] = t.astype(BF16)
            else:
                perm_ref[(g - 1) * chunks + dst // LANES, rows, :] = t

    def deinterleave(r0):
        for g, out_ref in ((1, qkv2_ref), (2, qkv3_ref)):
            d = ATTN_PATTERNS[g][1]
            n = SUB_ROWS // d
            for r in range(d):
                for ch in range(chunks):
                    rows = perm_ref[(g - 1) * chunks + ch, pl.ds(r0 + r, n, stride=d), :]
                    out_ref[0, r, r0 // d:r0 // d + n, ch * LANES:(ch + 1) * LANES] = rows.astype(BF16)

    h = normed(0)
    for r0 in range(0, tm, SUB_ROWS):
        pending = None
        for col in range(0, n_cols, WIDE_CHUNK):
            z = jnp.dot(h, w_ref[:, col:min(col + WIDE_CHUNK, n_cols)], preferred_element_type=F32)
            if pending is not None:
                emit_wide(r0, *pending)
            elif r0 > 0:
                deinterleave(r0 - SUB_ROWS)
            pending = (col, z)
        emit_wide(r0, *pending)
        if r0 + SUB_ROWS < tm:
            h = normed(r0 + SUB_ROWS)
    deinterleave(tm - SUB_ROWS)


def _in_proj(x, norm_g, w_in, rope_tabs):
    b, s, dm = x.shape
    tm = IN_TILE
    d2, d3 = ATTN_PATTERNS[1][1], ATTN_PATTERNS[2][1]
    qkv_w = 3 * GROUP_WIDTH
    const = lambda bi, i: (0, 0)
    tab_spec = pl.BlockSpec((tm, LANES), lambda bi, i: (i, 0))
    return pl.pallas_call(
        _in_proj_kernel,
        grid=(b, s // tm),
        in_specs=[
            pl.BlockSpec((1, tm, dm), lambda bi, i: (bi, i, 0)),
            pl.BlockSpec((1, dm), const),
            pl.BlockSpec(w_in.shape, const),
            tab_spec, tab_spec, tab_spec,
        ],
        out_specs=[
            pl.BlockSpec((1, tm, POOL_WIDTH), lambda bi, i: (bi, i, 0)),
            pl.BlockSpec((1, tm, qkv_w), lambda bi, i: (bi, i, 0)),
            pl.BlockSpec((1, d2, tm // d2, qkv_w), lambda bi, i: (bi, 0, i, 0)),
            pl.BlockSpec((1, d3, tm // d3, qkv_w), lambda bi, i: (bi, 0, i, 0)),
        ],
        out_shape=[
            jax.ShapeDtypeStruct((b, s, POOL_WIDTH), F32),
            jax.ShapeDtypeStruct((b, s, qkv_w), BF16),
            jax.ShapeDtypeStruct((b, d2, s // d2, qkv_w), BF16),
            jax.ShapeDtypeStruct((b, d3, s // d3, qkv_w), BF16),
        ],
        scratch_shapes=[pltpu.VMEM((2 * qkv_w // LANES, tm, LANES), F32)],
        compiler_params=pltpu.CompilerParams(
            dimension_semantics=("parallel", "parallel"), vmem_limit_bytes=VMEM_LIMIT_BYTES),
        name="in_proj",
    )(x, norm_g, w_in, *rope_tabs)


def _attend_chunk(g, units, nk, s_ref, p_ref, o_ref, m_ref, d_ref):
    low = lax.broadcasted_iota(jnp.int32, (Q_BLOCK, LANES), 1) < HEAD_DIM
    items = [(u, j) for u in range(len(units)) for j in range(PAIRS)]

    def scores(u, j):
        load = units[u][0]
        q2 = load(0, j)
        zero = jnp.zeros_like(q2)
        stacked = jnp.concatenate([jnp.where(low, q2, zero), jnp.where(low, zero, q2)], axis=0)
        s_ref[u, j, :, :nk] = lax.dot_general(stacked, load(1, j), (((1,), (1,)), ((), ())),
                                              preferred_element_type=F32)

    def softmax(u, j):
        _, limit, rows = units[u]
        tops, dens = [], []
        for head in range(2):
            hrows = slice(head * Q_BLOCK, (head + 1) * Q_BLOCK)
            s = jnp.minimum(s_ref[u, j, hrows, :nk], limit)
            m = jnp.max(s, axis=-1, keepdims=True)
            p = jnp.exp2(s - m)
            p_ref[u, j, hrows, :nk] = p.astype(BF16)
            tops.append(m)
            dens.append(jnp.sum(p, axis=-1, keepdims=True))
        m_ref[g, j, rows, :] = jnp.where(low, tops[0], tops[1])
        d_ref[g, j, rows, :] = jnp.where(low, dens[0], dens[1])

    def values(u, j):
        load, _, rows = units[u]
        o2 = jnp.dot(p_ref[u, j, :, :nk], load(2, j), preferred_element_type=F32)
        o_ref[g, j, rows, :] = jnp.where(low, o2[:Q_BLOCK], o2[Q_BLOCK:])

    if nk < KEY_SPAN:
        for stage in (scores, softmax, values):
            for item in items:
                stage(*item)
        return

    for step in range(len(items) + 2):
        if step < len(items):
            scores(*items[step])
        if 2 <= step:
            values(*items[step - 2])
        if 1 <= step <= len(items):
            softmax(*items[step - 1])


def _attn_kernel(qkv1_ref, qkv2_ref, qkv3_ref, out_ref, o_ref, m_ref, d_ref, s_ref, p_ref, lim_ref):
    seq = out_ref.shape[1]
    gw = GROUP_WIDTH
    scratch = (s_ref, p_ref, o_ref, m_ref, d_ref)

    rel = (lax.broadcasted_iota(jnp.int32, (Q_BLOCK, KEY_SPAN), 1)
           - lax.broadcasted_iota(jnp.int32, (Q_BLOCK, KEY_SPAN), 0))
    for variant in range(3):
        lim_ref[variant] = jnp.where(jnp.abs(rel - variant * BAND) <= BAND, jnp.inf, -jnp.inf)

    def limit_of(variant, nk):
        return lim_ref[variant, :, :nk]

    def cols(part, j):
        return slice(part * gw + j * LANES, part * gw + (j + 1) * LANES)

    def chunk0(c, carry):
        units = []
        for b in range(UNITS):
            blk = c * UNITS + b
            qs = pl.multiple_of(blk * Q_BLOCK, Q_BLOCK)
            variant = jnp.where(blk == 0, 0, jnp.where(blk == seq // Q_BLOCK - 1, 2, 1))
            ws = pl.multiple_of(qs - variant * BAND, BAND)

            def load(part, j, qs=qs, ws=ws):
                rows = pl.ds(qs, Q_BLOCK) if part == 0 else pl.ds(ws, KEY_SPAN)
                return qkv1_ref[0, rows, cols(part, j)]

            units.append((load, limit_of(variant, KEY_SPAN), pl.ds(qs, Q_BLOCK)))
        _attend_chunk(0, units, KEY_SPAN, *scratch)
        return carry

    lax.fori_loop(0, seq // (UNITS * Q_BLOCK), chunk0, 0)

    d2 = ATTN_PATTERNS[1][1]
    blocks2 = seq // d2 // Q_BLOCK
    assert UNITS % blocks2 == 0 and (d2 * blocks2) % UNITS == 0

    def chunk1(c, carry):
        units = []
        for b in range(UNITS):
            r = c * (UNITS // blocks2) + b // blocks2
            qs = (b % blocks2) * Q_BLOCK
            variant = 0 if qs == 0 else 2 if b % blocks2 == blocks2 - 1 else 1
            ws = qs - variant * BAND

            def load(part, j, r=r, qs=qs, ws=ws):
                rows = pl.ds(qs, Q_BLOCK) if part == 0 else pl.ds(ws, KEY_SPAN)
                return qkv2_ref[0, r, rows, cols(part, j)]

            units.append((load, limit_of(variant, KEY_SPAN), pl.ds(qs * d2 + r, Q_BLOCK, stride=d2)))
        _attend_chunk(1, units, KEY_SPAN, *scratch)
        return carry

    lax.fori_loop(0, d2 * blocks2 // UNITS, chunk1, 0)

    d3 = ATTN_PATTERNS[2][1]
    n3 = seq // d3

    def chunk2(c, carry):
        units = []
        for b in range(STAGED_UNITS):
            r = c * STAGED_UNITS + b
            units.append((lambda part, j, r=r: qkv3_ref[0, r, :, cols(part, j)], limit_of(0, n3),
                          pl.ds(r, n3, stride=d3)))
        _attend_chunk(2, units, n3, *scratch)
        return carry

    lax.fori_loop(0, d3 // STAGED_UNITS, chunk2, 0)

    def merge(i, carry):
        rows = pl.ds(pl.multiple_of(i * Q_BLOCK, Q_BLOCK), Q_BLOCK)
        for j in range(PAIRS):
            tops = [m_ref[g, j, rows, :] for g in range(N_GROUPS)]
            top = functools.reduce(jnp.maximum, tops)
            scales = [jnp.exp2(t - top) for t in tops]
            num = sum(e * o_ref[g, j, rows, :] for g, e in enumerate(scales))
            den = sum(e * d_ref[g, j, rows, :] for g, e in enumerate(scales))
            out_ref[0, rows, j * LANES:(j + 1) * LANES] = (num / den).astype(out_ref.dtype)
        return carry

    lax.fori_loop(0, seq // Q_BLOCK, merge, 0)


def _attention(qkv1, qkv2, qkv3):
    b, s, _ = qkv1.shape
    assert s // ATTN_PATTERNS[2][1] == Q_BLOCK and s % (UNITS * Q_BLOCK) == 0 and ATTN_PATTERNS[2][1] % STAGED_UNITS == 0
    assert all(w // 2 // d == BAND for w, d in ATTN_PATTERNS)
    return pl.pallas_call(
        _attn_kernel,
        grid=(b,),
        in_specs=[
            pl.BlockSpec((1,) + qkv1.shape[1:], lambda bi: (bi, 0, 0)),
            pl.BlockSpec((1,) + qkv2.shape[1:], lambda bi: (bi, 0, 0, 0)),
            pl.BlockSpec((1,) + qkv3.shape[1:], lambda bi: (bi, 0, 0, 0)),
        ],
        out_specs=pl.BlockSpec((1, s, GROUP_WIDTH), lambda bi: (bi, 0, 0)),
        out_shape=jax.ShapeDtypeStruct((b, s, GROUP_WIDTH), BF16),
        scratch_shapes=[
            pltpu.VMEM((N_GROUPS, PAIRS, s, LANES), F32),
            pltpu.VMEM((N_GROUPS, PAIRS, s, LANES), F32),
            pltpu.VMEM((N_GROUPS, PAIRS, s, LANES), F32),
            pltpu.VMEM((UNITS, PAIRS, 2 * Q_BLOCK, KEY_SPAN), F32),
            pltpu.VMEM((UNITS, PAIRS, 2 * Q_BLOCK, KEY_SPAN), BF16),
            pltpu.VMEM((3, Q_BLOCK, KEY_SPAN), F32),
        ],
        compiler_params=pltpu.CompilerParams(
            dimension_semantics=("parallel",), vmem_limit_bytes=VMEM_LIMIT_BYTES),
        name="attention",
    )(qkv1, qkv2, qkv3)


def _fold_pool_kernel(maps_ref, scale_ref, wup_ref, out_ref):
    for g in range(len(POOL_WINDOWS)):
        rows = slice(g * POOL_GROUP, (g + 1) * POOL_GROUP)
        out_ref[rows, :] = jnp.dot(maps_ref[g], scale_ref[rows, :] * wup_ref[rows, :], preferred_element_type=F32,
                                   precision=lax.Precision.HIGHEST).astype(out_ref.dtype)


def _fold_pool(maps, pool_scale, w_up_pool):
    return pl.pallas_call(
        _fold_pool_kernel,
        out_shape=jax.ShapeDtypeStruct(w_up_pool.shape, BF16),
        name="fold_pool",
    )(maps, pool_scale.reshape(POOL_WIDTH, 1), w_up_pool)


def _mix_kernel(x_ref, u_ref, up_ref, un_ref, at_ref, ic_ref, g_ref, wup_ref, wua_ref, wg_ref, bg_ref, wo_ref,
                out_ref, ext_ref):
    tm = x_ref.shape[1]
    dm = x_ref.shape[2]
    i = pl.program_id(1)
    ext_ref[0:POOL_HALO, :] = jnp.where(i > 0, up_ref[0], 0.0)
    ext_ref[POOL_HALO:POOL_HALO + tm, :] = u_ref[0]
    ext_ref[POOL_HALO + tm:, :] = jnp.where(i < pl.num_programs(1) - 1, un_ref[0], 0.0)

    n_chunks = dm // COL_CHUNK
    assert n_chunks == len(POOL_WINDOWS), "one pooling group per column chunk"

    def normed(r0):
        return _rms_scale(x_ref[0, r0:r0 + SUB_ROWS, :], g_ref[...]).astype(BF16)

    def pool_group(r0, j):
        w = POOL_WINDOWS[j]
        cols = slice(j * POOL_GROUP, (j + 1) * POOL_GROUP)
        span = SUB_ROWS + 2 * POOL_HALO
        ext = ext_ref[r0:r0 + span, cols]
        shifted = lambda a, k: pltpu.roll(a, span - k, 0)
        run, width = ext, 1
        while 2 * width < w:
            run, width = run + shifted(run, width), 2 * width
        start = POOL_HALO - w // 2
        assert 2 * width == w and start + width == POOL_HALO
        lead = run if start == 0 else shifted(run, start)
        total = lead[:SUB_ROWS] + run[POOL_HALO:POOL_HALO + SUB_ROWS]
        pooled = total * ic_ref[r0:r0 + SUB_ROWS, cols] - ext[POOL_HALO:POOL_HALO + SUB_ROWS]
        return pooled.astype(BF16)

    dot = functools.partial(jnp.dot, preferred_element_type=F32)

    def gate_dots(h, c):
        ca, cb = slice(c, c + COL_CHUNK), slice(dm + c, dm + c + COL_CHUNK)
        return dot(h, wg_ref[:, ca]) + bg_ref[:, ca], dot(h, wg_ref[:, cb]) + bg_ref[:, cb]

    def branch_dots(pooled, attn, c):
        return dot(pooled, wup_ref[:, c:c + COL_CHUNK]), dot(attn, wua_ref[:, c:c + COL_CHUNK])

    def gate(ga, gb, a, b):
        return (jax.nn.sigmoid(ga) * a + jax.nn.sigmoid(gb) * b).astype(BF16)

    h = normed(0)
    pooled = jnp.concatenate([pool_group(0, j) for j in range(n_chunks)], axis=-1)
    early = gate_dots(h, 0)
    for r0 in range(0, tm, SUB_ROWS):
        more = r0 + SUB_ROWS < tm
        attn = at_ref[0, r0:r0 + SUB_ROWS, :]
        ys, pending, upcoming = [], None, []
        for j in range(n_chunks):
            front = early if j == 0 else gate_dots(h, j * COL_CHUNK)
            parts = front + branch_dots(pooled, attn, j * COL_CHUNK)
            if pending is not None:
                ys.append(gate(*pending))
            pending = parts
            if more:
                upcoming.append(pool_group(r0 + SUB_ROWS, j))
        if more:
            h = normed(r0 + SUB_ROWS)
            early = gate_dots(h, 0)
        ys.append(gate(*pending))
        delta = dot(jnp.concatenate(ys, axis=-1), wo_ref[...])
        if more:
            pooled = jnp.concatenate(upcoming, axis=-1)
        out_ref[0, r0:r0 + SUB_ROWS, :] = x_ref[0, r0:r0 + SUB_ROWS, :] + delta


def _inverse_counts(seq):
    pos = np.arange(seq)
    cols = [1.0 / (np.minimum(pos + w // 2, seq) - np.maximum(pos - w // 2, 0)) for w in POOL_WINDOWS]
    return jnp.asarray(np.repeat(np.stack(cols, axis=1), POOL_GROUP, axis=1), dtype=F32)


def _mix(x, u, attn, norm_g, w_up_pool, w_up_attn, w_gate, b_gate, w_out):
    b, s, dm = x.shape
    tm = TOKEN_TILE
    halo_blocks = tm // POOL_HALO
    last_halo = s // POOL_HALO - 1
    tile = lambda bi, i: (bi, i, 0)
    const2 = lambda bi, i: (0, 0)
    return pl.pallas_call(
        _mix_kernel,
        grid=(b, s // tm),
        in_specs=[
            pl.BlockSpec((1, tm, dm), tile),
            pl.BlockSpec((1, tm, POOL_WIDTH), tile),
            pl.BlockSpec((1, POOL_HALO, POOL_WIDTH), lambda bi, i: (bi, jnp.maximum(i * halo_blocks - 1, 0), 0)),
            pl.BlockSpec((1, POOL_HALO, POOL_WIDTH),
                         lambda bi, i: (bi, jnp.minimum((i + 1) * halo_blocks, last_halo), 0)),
            pl.BlockSpec((1, tm, GROUP_WIDTH), tile),
            pl.BlockSpec((tm, POOL_WIDTH), lambda bi, i: (i, 0)),
            pl.BlockSpec((1, dm), const2),
            pl.BlockSpec(w_up_pool.shape, const2),
            pl.BlockSpec(w_up_attn.shape, const2),
            pl.BlockSpec(w_gate.shape, const2),
            pl.BlockSpec((1, 2 * dm), const2),
            pl.BlockSpec(w_out.shape, const2),
        ],
        out_specs=pl.BlockSpec((1, tm, dm), tile),
        out_shape=jax.ShapeDtypeStruct((b, s, dm), F32),
        scratch_shapes=[pltpu.VMEM((tm + 2 * POOL_HALO, POOL_WIDTH), F32)],
        compiler_params=pltpu.CompilerParams(
            dimension_semantics=("parallel", "parallel"), vmem_limit_bytes=VMEM_LIMIT_BYTES),
        name="mix",
    )(x, u, u, u, attn, _inverse_counts(s), norm_g, w_up_pool, w_up_attn, w_gate, b_gate, w_out)


def _ffn_kernel(x_ref, g_ref, wg_ref, wu_ref, wd_ref, gf_ref, out_ref):
    tm = x_ref.shape[0]
    d_ff = wg_ref.shape[1]
    dot = functools.partial(jnp.dot, preferred_element_type=F32)

    def normed(r0):
        return _rms_scale(x_ref[r0:r0 + SUB_ROWS, :], g_ref[...]).astype(BF16)

    def swish_gate(parts):
        gate, up = parts
        return (gate * jax.nn.sigmoid(gate) * up).astype(BF16)

    def finish(r0, delta):
        out_ref[r0:r0 + SUB_ROWS, :] = _rms_scale(x_ref[r0:r0 + SUB_ROWS, :] + delta, gf_ref[...])

    h = normed(0)
    unfinished = None
    for r0 in range(0, tm, SUB_ROWS):
        acts, pending = [], None
        for c in range(0, d_ff, COL_CHUNK):
            cols = slice(c, c + COL_CHUNK)
            parts = (dot(h, wg_ref[:, cols]), dot(h, wu_ref[:, cols]))
            if pending is not None:
                acts.append(swish_gate(pending))
            elif unfinished is not None:
                finish(*unfinished)
            pending = parts
        acts.append(swish_gate(pending))
        if r0 + SUB_ROWS < tm:
            h = normed(r0 + SUB_ROWS)
        unfinished = (r0, dot(jnp.concatenate(acts, axis=-1), wd_ref[...]))
    finish(*unfinished)


def _ffn(x, norm_g, w_gate, w_up, w_down, norm_final):
    n, dm = x.shape
    tm = FFN_TILE
    const = lambda i: (0, 0)
    resident = functools.partial(pl.BlockSpec, index_map=const, pipeline_mode=pl.Buffered(1))
    return pl.pallas_call(
        _ffn_kernel,
        grid=(n // tm,),
        in_specs=[
            pl.BlockSpec((tm, dm), lambda i: (i, 0)),
            pl.BlockSpec((1, dm), const),
            resident(w_gate.shape),
            resident(w_up.shape),
            resident(w_down.shape),
            pl.BlockSpec((1, dm), const),
        ],
        out_specs=pl.BlockSpec((tm, dm), lambda i: (i, 0)),
        out_shape=jax.ShapeDtypeStruct((n, dm), F32),
        compiler_params=pltpu.CompilerParams(
            dimension_semantics=("parallel",), vmem_limit_bytes=VMEM_LIMIT_BYTES),
        name="ffn",
    )(x, norm_g, w_gate, w_up, w_down, norm_final)


def _encoder(x, p, rope_tabs):
    b, s, dm = x.shape
    u, qkv1, qkv2, qkv3 = _in_proj(x, p["norm_mix"], p["w_in"], rope_tabs)
    attn = _attention(qkv1, qkv2, qkv3)
    x1 = _mix(x, u, attn, p["norm_mix"], p["w_pool"], p["w_up_attn"], p["w_gate"], p["b_gate"], p["w_out"])
    y = _ffn(x1.reshape(b * s, dm), p["norm_ffn"], p["w_ffn_gate"], p["w_ffn_up"], p["w_ffn_down"], p["norm_final"])
    return y.reshape(b, s, dm)


def kernel(x_prompt, x_sample, norm_mix, w_in, pool_maps, pool_scale, w_up_pool, w_up_attn, w_gate, b_gate,
           w_out, norm_ffn, w_ffn_gate, w_ffn_up, w_ffn_down, norm_final):
    assert norm_mix.shape[0] == 1, "one encoder layer"
    p = {
        "norm_mix": norm_mix[0][None, :],
        "w_in": w_in[0].astype(BF16),
        "w_pool": _fold_pool(pool_maps[0], pool_scale[0], w_up_pool[0]),
        "w_up_attn": w_up_attn[0].astype(BF16),
        "w_gate": w_gate[0].astype(BF16),
        "b_gate": b_gate[0][None, :],
        "w_out": w_out[0].astype(BF16),
        "norm_ffn": norm_ffn[0][None, :],
        "w_ffn_gate": w_ffn_gate[0].astype(BF16),
        "w_ffn_up": w_ffn_up[0].astype(BF16),
        "w_ffn_down": w_ffn_down[0].astype(BF16),
        "norm_final": norm_final[None, :],
    }
    outs = []
    for x in (x_prompt, x_sample):
        rope_tabs = _rope_tables(x.shape[1])
        outs.append(_encoder(x, p, rope_tabs))
    return tuple(outs)
```

```python
import functools

import jax
import jax.numpy as jnp
import numpy as np
from jax import lax
from jax.experimental import pallas as pl
from jax.experimental.pallas import tpu as pltpu

HEAD_DIM = 64
HEADS_PER_GROUP = 4
ATTN_PATTERNS = ((128, 1), (512, 4), (2048, 16))
N_GROUPS = len(ATTN_PATTERNS)
GROUP_WIDTH = HEADS_PER_GROUP * HEAD_DIM
ATTN_WIDTH = N_GROUPS * GROUP_WIDTH
ROPE_DIM = HEAD_DIM // 4
ROPE_THETA = 500000.0
POOL_WINDOWS = (2, 4, 8, 16)
POOL_GROUP = 128
POOL_WIDTH = POOL_GROUP * len(POOL_WINDOWS)
POOL_HALO = 8
NORM_EPS = 1e-6

LANES = 128
Q_BLOCK = 128
BAND = 64
KEY_SPAN = Q_BLOCK + 2 * BAND
PAIRS = GROUP_WIDTH // LANES
UNITS = 16
STAGED_UNITS = 4
TOKEN_TILE = 1024
IN_TILE = 1024
FFN_TILE = 1024
SUB_ROWS = 256
COL_CHUNK = 256
WIDE_CHUNK = 512
VMEM_LIMIT_BYTES = 56 * 1024 * 1024

F32 = jnp.float32
BF16 = jnp.bfloat16


def _rms_scale(x, gain):
    return x * lax.rsqrt(jnp.mean(x * x, axis=-1, keepdims=True) + NORM_EPS) * gain


def _rope_tables(seq):
    half = ROPE_DIM // 2
    inv = (ROPE_THETA ** (-np.arange(half, dtype=np.float32) / half)).astype(np.float32).astype(np.float64)
    ang = np.arange(seq, dtype=np.float64)[:, None] * inv[None, :]
    cos, sin = np.cos(ang), np.sin(ang)
    c = np.ones((seq, HEAD_DIM))
    s_up = np.zeros((seq, HEAD_DIM))
    s_down = np.zeros((seq, HEAD_DIM))
    c[:, :half] = cos
    c[:, half:ROPE_DIM] = cos
    s_down[:, :half] = -sin
    s_up[:, half:ROPE_DIM] = sin
    rep = LANES // HEAD_DIM
    return tuple(jnp.asarray(np.tile(t, (1, rep)), dtype=F32) for t in (c, s_up, s_down))


def _in_proj_kernel(x_ref, g_ref, w_ref, c_ref, su_ref, sd_ref, u_ref, qkv1_ref, qkv2_ref, qkv3_ref, perm_ref):
    tm = x_ref.shape[1]
    n_cols = w_ref.shape[1]
    assert POOL_WIDTH % GROUP_WIDTH == 0 and WIDE_CHUNK % GROUP_WIDTH == 0
    half = ROPE_DIM // 2
    scale = HEAD_DIM ** -0.5 * np.log2(np.e)
    chunks = 3 * GROUP_WIDTH // LANES

    def normed(r0):
        return _rms_scale(x_ref[0, r0:r0 + SUB_ROWS, :], g_ref[...]).astype(BF16)

    def emit_wide(r0, col, z):
        for off in range(0, z.shape[1], GROUP_WIDTH):
            emit(r0, col + off, z[:, off:off + GROUP_WIDTH])

    def emit(r0, col, z):
        rows = slice(r0, r0 + SUB_ROWS)
        if col < POOL_WIDTH:
            u_ref[0, rows, col:col + GROUP_WIDTH] = z
            return
        part, g = divmod((col - POOL_WIDTH) // GROUP_WIDTH, N_GROUPS)
        for j in range(GROUP_WIDTH // LANES):
            t = z[:, j * LANES:(j + 1) * LANES]
            if part < 2:
                t = (t * c_ref[rows, :] + pltpu.roll(t, half, 1) * su_ref[rows, :]
                     + pltpu.roll(t, LANES - half, 1) * sd_ref[rows, :])
            if part == 0:
                t = t * scale
            dst = part * GROUP_WIDTH + j * LANES
            if g == 0:
                qkv1_ref[0, rows, dst:dst + LANES] = t.astype(BF16)
            else:
                perm_ref[(g - 1) * chunks + dst // LANES, rows, :] = t

    def deinterleave(r0):
        for g, out_ref in ((1, qkv2_ref), (2, qkv3_ref)):
            d = ATTN_PATTERNS[g][1]
            n = SUB_ROWS // d
            for r in range(d):
                for ch in range(chunks):
                    rows = perm_ref[(g - 1) * chunks + ch, pl.ds(r0 + r, n, stride=d), :]
                    out_ref[0, r, r0 // d:r0 // d + n, ch * LANES:(ch + 1) * LANES] = rows.astype(BF16)

    h = normed(0)
    for r0 in range(0, tm, SUB_ROWS):
        pending = None
        for col in range(0, n_cols, WIDE_CHUNK):
            z = jnp.dot(h, w_ref[:, col:min(col + WIDE_CHUNK, n_cols)], preferred_element_type=F32)
            if pending is not None:
                emit_wide(r0, *pending)
            elif r0 > 0:
                deinterleave(r0 - SUB_ROWS)
            pending = (col, z)
        emit_wide(r0, *pending)
        if r0 + SUB_ROWS < tm:
            h = normed(r0 + SUB_ROWS)
    deinterleave(tm - SUB_ROWS)


def _in_proj(x, norm_g, w_in, rope_tabs):
    b, s, dm = x.shape
    tm = IN_TILE
    d2, d3 = ATTN_PATTERNS[1][1], ATTN_PATTERNS[2][1]
    qkv_w = 3 * GROUP_WIDTH
    const = lambda bi, i: (0, 0)
    tab_spec = pl.BlockSpec((tm, LANES), lambda bi, i: (i, 0))
    return pl.pallas_call(
        _in_proj_kernel,
        grid=(b, s // tm),
        in_specs=[
            pl.BlockSpec((1, tm, dm), lambda bi, i: (bi, i, 0)),
            pl.BlockSpec((1, dm), const),
            pl.BlockSpec(w_in.shape, const),
            tab_spec, tab_spec, tab_spec,
        ],
        out_specs=[
            pl.BlockSpec((1, tm, POOL_WIDTH), lambda bi, i: (bi, i, 0)),
            pl.BlockSpec((1, tm, qkv_w), lambda bi, i: (bi, i, 0)),
            pl.BlockSpec((1, d2, tm // d2, qkv_w), lambda bi, i: (bi, 0, i, 0)),
            pl.BlockSpec((1, d3, tm // d3, qkv_w), lambda bi, i: (bi, 0, i, 0)),
        ],
        out_shape=[
            jax.ShapeDtypeStruct((b, s, POOL_WIDTH), F32),
            jax.ShapeDtypeStruct((b, s, qkv_w), BF16),
            jax.ShapeDtypeStruct((b, d2, s // d2, qkv_w), BF16),
            jax.ShapeDtypeStruct((b, d3, s // d3, qkv_w), BF16),
        ],
        scratch_shapes=[pltpu.VMEM((2 * qkv_w // LANES, tm, LANES), F32)],
        compiler_params=pltpu.CompilerParams(
            dimension_semantics=("parallel", "parallel"), vmem_limit_bytes=VMEM_LIMIT_BYTES),
        name="in_proj",
    )(x, norm_g, w_in, *rope_tabs)


def _attend_chunk(g, units, nk, s_ref, p_ref, o_ref, m_ref, d_ref):
    low = lax.broadcasted_iota(jnp.int32, (Q_BLOCK, LANES), 1) < HEAD_DIM
    items = [(u, j) for u in range(len(units)) for j in range(PAIRS)]

    def scores(u, j):
        load = units[u][0]
        q2 = load(0, j)
        zero = jnp.zeros_like(q2)
        stacked = jnp.concatenate([jnp.where(low, q2, zero), jnp.where(low, zero, q2)], axis=0)
        s_ref[u, j, :, :nk] = lax.dot_general(stacked, load(1, j), (((1,), (1,)), ((), ())),
                                              preferred_element_type=F32)

    def softmax(u, j):
        _, limit, rows = units[u]
        tops, dens = [], []
        for head in range(2):
            hrows = slice(head * Q_BLOCK, (head + 1) * Q_BLOCK)
            s = jnp.minimum(s_ref[u, j, hrows, :nk], limit)
            m = jnp.max(s, axis=-1, keepdims=True)
            p = jnp.exp2(s - m)
            p_ref[u, j, hrows, :nk] = p.astype(BF16)
            tops.append(m)
            dens.append(jnp.sum(p, axis=-1, keepdims=True))
        m_ref[g, j, rows, :] = jnp.where(low, tops[0], tops[1])
        d_ref[g, j, rows, :] = jnp.where(low, dens[0], dens[1])

    def values(u, j):
        load, _, rows = units[u]
        o2 = jnp.dot(p_ref[u, j, :, :nk], load(2, j), preferred_element_type=F32)
        o_ref[g, j, rows, :] = jnp.where(low, o2[:Q_BLOCK], o2[Q_BLOCK:])

    if nk < KEY_SPAN:
        for stage in (scores, softmax, values):
            for item in items:
                stage(*item)
        return

    for step in range(len(items) + 2):
        if step < len(items):
            scores(*items[step])
        if 2 <= step:
            values(*items[step - 2])
        if 1 <= step <= len(items):
            softmax(*items[step - 1])


def _attn_kernel(qkv1_ref, qkv2_ref, qkv3_ref, out_ref, o_ref, m_ref, d_ref, s_ref, p_ref, lim_ref):
    seq = out_ref.shape[1]
    gw = GROUP_WIDTH
    scratch = (s_ref, p_ref, o_ref, m_ref, d_ref)

    rel = (lax.broadcasted_iota(jnp.int32, (Q_BLOCK, KEY_SPAN), 1)
           - lax.broadcasted_iota(jnp.int32, (Q_BLOCK, KEY_SPAN), 0))
    for variant in range(3):
        lim_ref[variant] = jnp.where(jnp.abs(rel - variant * BAND) <= BAND, jnp.inf, -jnp.inf)

    def limit_of(variant, nk):
        return lim_ref[variant, :, :nk]

    def cols(part, j):
        return slice(part * gw + j * LANES, part * gw + (j + 1) * LANES)

    def chunk0(c, carry):
        units = []
        for b in range(UNITS):
            blk = c * UNITS + b
            qs = pl.multiple_of(blk * Q_BLOCK, Q_BLOCK)
            variant = jnp.where(blk == 0, 0, jnp.where(blk == seq // Q_BLOCK - 1, 2, 1))
            ws = pl.multiple_of(qs - variant * BAND, BAND)

            def load(part, j, qs=qs, ws=ws):
                rows = pl.ds(qs, Q_BLOCK) if part == 0 else pl.ds(ws, KEY_SPAN)
                return qkv1_ref[0, rows, cols(part, j)]

            units.append((load, limit_of(variant, KEY_SPAN), pl.ds(qs, Q_BLOCK)))
        _attend_chunk(0, units, KEY_SPAN, *scratch)
        return carry

    lax.fori_loop(0, seq // (UNITS * Q_BLOCK), chunk0, 0)

    d2 = ATTN_PATTERNS[1][1]
    blocks2 = seq // d2 // Q_BLOCK
    assert UNITS % blocks2 == 0 and (d2 * blocks2) % UNITS == 0

    def chunk1(c, carry):
        units = []
        for b in range(UNITS):
            r = c * (UNITS // blocks2) + b // blocks2
            qs = (b % blocks2) * Q_BLOCK
            variant = 0 if qs == 0 else 2 if b % blocks2 == blocks2 - 1 else 1
            ws = qs - variant * BAND

            def load(part, j, r=r, qs=qs, ws=ws):
                rows = pl.ds(qs, Q_BLOCK) if part == 0 else pl.ds(ws, KEY_SPAN)
                return qkv2_ref[0, r, rows, cols(part, j)]

            units.append((load, limit_of(variant, KEY_SPAN), pl.ds(qs * d2 + r, Q_BLOCK, stride=d2)))
        _attend_chunk(1, units, KEY_SPAN, *scratch)
        return carry

    lax.fori_loop(0, d2 * blocks2 // UNITS, chunk1, 0)

    d3 = ATTN_PATTERNS[2][1]
    n3 = seq // d3

    def chunk2(c, carry):
        units = []
        for b in range(STAGED_UNITS):
            r = c * STAGED_UNITS + b
            units.append((lambda part, j, r=r: qkv3_ref[0, r, :, cols(part, j)], limit_of(0, n3),
                          pl.ds(r, n3, stride=d3)))
        _attend_chunk(2, units, n3, *scratch)
        return carry

    lax.fori_loop(0, d3 // STAGED_UNITS, chunk2, 0)

    def merge(i, carry):
        rows = pl.ds(pl.multiple_of(i * Q_BLOCK, Q_BLOCK), Q_BLOCK)
        for j in range(PAIRS):
            tops = [m_ref[g, j, rows, :] for g in range(N_GROUPS)]
            top = functools.reduce(jnp.maximum, tops)
            scales = [jnp.exp2(t - top) for t in tops]
            num = sum(e * o_ref[g, j, rows, :] for g, e in enumerate(scales))
            den = sum(e * d_ref[g, j, rows, :] for g, e in enumerate(scales))
            out_ref[0, rows, j * LANES:(j + 1) * LANES] = (num / den).astype(out_ref.dtype)
        return carry

    lax.fori_loop(0, seq // Q_BLOCK, merge, 0)


def _attention(qkv1, qkv2, qkv3):
    b, s, _ = qkv1.shape
    assert s // ATTN_PATTERNS[2][1] == Q_BLOCK and s % (UNITS * Q_BLOCK) == 0 and ATTN_PATTERNS[2][1] % STAGED_UNITS == 0
    assert all(w // 2 // d == BAND for w, d in ATTN_PATTERNS)
    return pl.pallas_call(
        _attn_kernel,
        grid=(b,),
        in_specs=[
            pl.BlockSpec((1,) + qkv1.shape[1:], lambda bi: (bi, 0, 0)),
            pl.BlockSpec((1,) + qkv2.shape[1:], lambda bi: (bi, 0, 0, 0)),
            pl.BlockSpec((1,) + qkv3.shape[1:], lambda bi: (bi, 0, 0, 0)),
        ],
        out_specs=pl.BlockSpec((1, s, GROUP_WIDTH), lambda bi: (bi, 0, 0)),
        out_shape=jax.ShapeDtypeStruct((b, s, GROUP_WIDTH), BF16),
        scratch_shapes=[
            pltpu.VMEM((N_GROUPS, PAIRS, s, LANES), F32),
            pltpu.VMEM((N_GROUPS, PAIRS, s, LANES), F32),
            pltpu.VMEM((N_GROUPS, PAIRS, s, LANES), F32),
            pltpu.VMEM((UNITS, PAIRS, 2 * Q_BLOCK, KEY_SPAN), F32),
            pltpu.VMEM((UNITS, PAIRS, 2 * Q_BLOCK, KEY_SPAN), BF16),
            pltpu.VMEM((3, Q_BLOCK, KEY_SPAN), F32),
        ],
        compiler_params=pltpu.CompilerParams(
            dimension_semantics=("parallel",), vmem_limit_bytes=VMEM_LIMIT_BYTES),
        name="attention",
    )(qkv1, qkv2, qkv3)


def _fold_pool_kernel(maps_ref, scale_ref, wup_ref, out_ref):
    for g in range(len(POOL_WINDOWS)):
        rows = slice(g * POOL_GROUP, (g + 1) * POOL_GROUP)
        out_ref[rows, :] = jnp.dot(maps_ref[g], scale_ref[rows, :] * wup_ref[rows, :], preferred_element_type=F32,
                                   precision=lax.Precision.HIGHEST).astype(out_ref.dtype)


def _fold_pool(maps, pool_scale, w_up_pool):
    return pl.pallas_call(
        _fold_pool_kernel,
        out_shape=jax.ShapeDtypeStruct(w_up_pool.shape, BF16),
        name="fold_pool",
    )(maps, pool_scale.reshape(POOL_WIDTH, 1), w_up_pool)


def _mix_kernel(x_ref, u_ref, up_ref, un_ref, at_ref, ic_ref, g_ref, wup_ref, wua_ref, wg_ref, bg_ref, wo_ref,
                out_ref, ext_ref):
    tm = x_ref.shape[1]
    dm = x_ref.shape[2]
    i = pl.program_id(1)
    ext_ref[0:POOL_HALO, :] = jnp.where(i > 0, up_ref[0], 0.0)
    ext_ref[POOL_HALO:POOL_HALO + tm, :] = u_ref[0]
    ext_ref[POOL_HALO + tm:, :] = jnp.where(i < pl.num_programs(1) - 1, un_ref[0], 0.0)

    n_chunks = dm // COL_CHUNK
    assert n_chunks == len(POOL_WINDOWS), "one pooling group per column chunk"

    def normed(r0):
        return _rms_scale(x_ref[0, r0:r0 + SUB_ROWS, :], g_ref[...]).astype(BF16)

    def pool_group(r0, j):
        w = POOL_WINDOWS[j]
        cols = slice(j * POOL_GROUP, (j + 1) * POOL_GROUP)
        span = SUB_ROWS + 2 * POOL_HALO
        ext = ext_ref[r0:r0 + span, cols]
        shifted = lambda a, k: pltpu.roll(a, span - k, 0)
        run, width = ext, 1
        while 2 * width < w:
            run, width = run + shifted(run, width), 2 * width
        start = POOL_HALO - w // 2
        assert 2 * width == w and start + width == POOL_HALO
        lead = run if start == 0 else shifted(run, start)
        total = lead[:SUB_ROWS] + run[POOL_HALO:POOL_HALO + SUB_ROWS]
        pooled = total * ic_ref[r0:r0 + SUB_ROWS, cols] - ext[POOL_HALO:POOL_HALO + SUB_ROWS]
        return pooled.astype(BF16)

    dot = functools.partial(jnp.dot, preferred_element_type=F32)

    def gate_dots(h, c):
        ca, cb = slice(c, c + COL_CHUNK), slice(dm + c, dm + c + COL_CHUNK)
        return dot(h, wg_ref[:, ca]) + bg_ref[:, ca], dot(h, wg_ref[:, cb]) + bg_ref[:, cb]

    def branch_dots(pooled, attn, c):
        return dot(pooled, wup_ref[:, c:c + COL_CHUNK]), dot(attn, wua_ref[:, c:c + COL_CHUNK])

    def gate(ga, gb, a, b):
        return (jax.nn.sigmoid(ga) * a + jax.nn.sigmoid(gb) * b).astype(BF16)

    h = normed(0)
    pooled = jnp.concatenate([pool_group(0, j) for j in range(n_chunks)], axis=-1)
    early = gate_dots(h, 0)
    for r0 in range(0, tm, SUB_ROWS):
        more = r0 + SUB_ROWS < tm
        attn = at_ref[0, r0:r0 + SUB_ROWS, :]
        ys, pending, upcoming = [], None, []
        for j in range(n_chunks):
            front = early if j == 0 else gate_dots(h, j * COL_CHUNK)
            parts = front + branch_dots(pooled, attn, j * COL_CHUNK)
            if pending is not None:
                ys.append(gate(*pending))
            pending = parts
            if more:
                upcoming.append(pool_group(r0 + SUB_ROWS, j))
        if more:
            h = normed(r0 + SUB_ROWS)
            early = gate_dots(h, 0)
        ys.append(gate(*pending))
        delta = dot(jnp.concatenate(ys, axis=-1), wo_ref[...])
        if more:
            pooled = jnp.concatenate(upcoming, axis=-1)
        out_ref[0, r0:r0 + SUB_ROWS, :] = x_ref[0, r0:r0 + SUB_ROWS, :] + delta


def _inverse_counts(seq):
    pos = np.arange(seq)
    cols = [1.0 / (np.minimum(pos + w // 2, seq) - np.maximum(pos - w // 2, 0)) for w in POOL_WINDOWS]
    return jnp.asarray(np.repeat(np.stack(cols, axis=1), POOL_GROUP, axis=1), dtype=F32)


def _mix(x, u, attn, norm_g, w_up_pool, w_up_attn, w_gate, b_gate, w_out):
    b, s, dm = x.shape
    tm = TOKEN_TILE
    halo_blocks = tm // POOL_HALO
    last_halo = s // POOL_HALO - 1
    tile = lambda bi, i: (bi, i, 0)
    const2 = lambda bi, i: (0, 0)
    return pl.pallas_call(
        _mix_kernel,
        grid=(b, s // tm),
        in_specs=[
            pl.BlockSpec((1, tm, dm), tile),
            pl.BlockSpec((1, tm, POOL_WIDTH), tile),
            pl.BlockSpec((1, POOL_HALO, POOL_WIDTH), lambda bi, i: (bi, jnp.maximum(i * halo_blocks - 1, 0), 0)),
            pl.BlockSpec((1, POOL_HALO, POOL_WIDTH),
                         lambda bi, i: (bi, jnp.minimum((i + 1) * halo_blocks, last_halo), 0)),
            pl.BlockSpec((1, tm, GROUP_WIDTH), tile),
            pl.BlockSpec((tm, POOL_WIDTH), lambda bi, i: (i, 0)),
            pl.BlockSpec((1, dm), const2),
            pl.BlockSpec(w_up_pool.shape, const2),
            pl.BlockSpec(w_up_attn.shape, const2),
            pl.BlockSpec(w_gate.shape, const2),
            pl.BlockSpec((1, 2 * dm), const2),
            pl.BlockSpec(w_out.shape, const2),
        ],
        out_specs=pl.BlockSpec((1, tm, dm), tile),
        out_shape=jax.ShapeDtypeStruct((b, s, dm), F32),
        scratch_shapes=[pltpu.VMEM((tm + 2 * POOL_HALO, POOL_WIDTH), F32)],
        compiler_params=pltpu.CompilerParams(
            dimension_semantics=("parallel", "parallel"), vmem_limit_bytes=VMEM_LIMIT_BYTES),
        name="mix",
    )(x, u, u, u, attn, _inverse_counts(s), norm_g, w_up_pool, w_up_attn, w_gate, b_gate, w_out)


def _ffn_kernel(x_ref, g_ref, wg_ref, wu_ref, wd_ref, gf_ref, out_ref):
    tm = x_ref.shape[0]
    d_ff = wg_ref.shape[1]
    dot = functools.partial(jnp.dot, preferred_element_type=F32)

    def normed(r0):
        return _rms_scale(x_ref[r0:r0 + SUB_ROWS, :], g_ref[...]).astype(BF16)

    def swish_gate(parts):
        gate, up = parts
        return (gate * jax.nn.sigmoid(gate) * up).astype(BF16)

    def finish(r0, delta):
        out_ref[r0:r0 + SUB_ROWS, :] = _rms_scale(x_ref[r0:r0 + SUB_ROWS, :] + delta, gf_ref[...])

    def chunk_dots(h, c):
        return dot(h, wg_ref[:, c:c + COL_CHUNK]), dot(h, wu_ref[:, c:c + COL_CHUNK])

    h = normed(0)
    early, unfinished = chunk_dots(h, 0), None
    for r0 in range(0, tm, SUB_ROWS):
        acts, pending = [], None
        for c in range(0, d_ff, COL_CHUNK):
            parts = early if c == 0 else chunk_dots(h, c)
            if pending is not None:
                acts.append(swish_gate(pending))
            elif unfinished is not None:
                finish(*unfinished)
            pending = parts
        acts.append(swish_gate(pending))
        if r0 + SUB_ROWS < tm:
            h = normed(r0 + SUB_ROWS)
            early = chunk_dots(h, 0)
        unfinished = (r0, dot(jnp.concatenate(acts, axis=-1), wd_ref[...]))
    finish(*unfinished)


def _ffn(x, norm_g, w_gate, w_up, w_down, norm_final):
    n, dm = x.shape
    tm = FFN_TILE
    const = lambda i: (0, 0)
    resident = functools.partial(pl.BlockSpec, index_map=const, pipeline_mode=pl.Buffered(1))
    return pl.pallas_call(
        _ffn_kernel,
        grid=(n // tm,),
        in_specs=[
            pl.BlockSpec((tm, dm), lambda i: (i, 0)),
            pl.BlockSpec((1, dm), const),
            resident(w_gate.shape),
            resident(w_up.shape),
            resident(w_down.shape),
            pl.BlockSpec((1, dm), const),
        ],
        out_specs=pl.BlockSpec((tm, dm), lambda i: (i, 0)),
        out_shape=jax.ShapeDtypeStruct((n, dm), F32),
        compiler_params=pltpu.CompilerParams(
            dimension_semantics=("parallel",), vmem_limit_bytes=VMEM_LIMIT_BYTES),
        name="ffn",
    )(x, norm_g, w_gate, w_up, w_down, norm_final)


def _encoder(x, p, rope_tabs):
    b, s, dm = x.shape
    u, qkv1, qkv2, qkv3 = _in_proj(x, p["norm_mix"], p["w_in"], rope_tabs)
    attn = _attention(qkv1, qkv2, qkv3)
    x1 = _mix(x, u, attn, p["norm_mix"], p["w_pool"], p["w_up_attn"], p["w_gate"], p["b_gate"], p["w_out"])
    y = _ffn(x1.reshape(b * s, dm), p["norm_ffn"], p["w_ffn_gate"], p["w_ffn_up"], p["w_ffn_down"], p["norm_final"])
    return y.reshape(b, s, dm)


def kernel(x_prompt, x_sample, norm_mix, w_in, pool_maps, pool_scale, w_up_pool, w_up_attn, w_gate, b_gate,
           w_out, norm_ffn, w_ffn_gate, w_ffn_up, w_ffn_down, norm_final):
    assert norm_mix.shape[0] == 1, "one encoder layer"
    p = {
        "norm_mix": norm_mix[0][None, :],
        "w_in": w_in[0].astype(BF16),
        "w_pool": _fold_pool(pool_maps[0], pool_scale[0], w_up_pool[0]),
        "w_up_attn": w_up_attn[0].astype(BF16),
        "w_gate": w_gate[0].astype(BF16),
        "b_gate": b_gate[0][None, :],
        "w_out": w_out[0].astype(BF16),
        "norm_ffn": norm_ffn[0][None, :],
        "w_ffn_gate": w_ffn_gate[0].astype(BF16),
        "w_ffn_up": w_ffn_up[0].astype(BF16),
        "w_ffn_down": w_ffn_down[0].astype(BF16),
        "norm_final": norm_final[None, :],
    }
    outs = []
    for x in (x_prompt, x_sample):
        rope_tabs = _rope_tables(x.shape[1])
        outs.append(_encoder(x, p, rope_tabs))
    return tuple(outs)
```
